```python
import jax
import jax.numpy as jnp
from jax import lax
import numpy as np

D_MODEL = 1024
BATCH = 2
SEQ = 8192
DEPTH = 2

CTX_LEN = 256
GRID_W = 64
CONV_DIM = 512
CONV_K = 31
NA_HEADS = 8
HEAD_DIM = 64
NA_DIM = NA_HEADS * HEAD_DIM
NA_KH_MAX = 8
NA_KW = 16
ROPE_THETA = 10000.0
FFN_DIM = 2816
FFN_CONV_K = 3
N_BRANCH = 2
EPS = 1e-6
GLU_END = 2 * CONV_DIM
QKV_END = GLU_END + 3 * NA_DIM
IN_DIM = QKV_END + N_BRANCH * D_MODEL

kernel_name = "hybrid_conformer_natten_dit_trunk"


def rmsnorm(x, g):
    xf = x.astype(jnp.float32)
    y = xf * lax.rsqrt(jnp.mean(xf * xf, axis=-1, keepdims=True) + EPS)
    return (y * g.astype(jnp.float32)).astype(x.dtype)


def layernorm(x, g, b):
    xf = x.astype(jnp.float32)
    mu = jnp.mean(xf, axis=-1, keepdims=True)
    var = jnp.mean(jnp.square(xf - mu), axis=-1, keepdims=True)
    y = (xf - mu) * lax.rsqrt(var + EPS)
    return (y * g.astype(jnp.float32) + b.astype(jnp.float32)).astype(x.dtype)


def modulate(h, shift, scale):
    return h * (1 + scale) + shift


def dwconv1d(x, w, b):
    k, ch = w.shape
    y = lax.conv_general_dilated(
        x, w.astype(x.dtype)[:, None, :], window_strides=(1,),
        padding=[(k // 2, k // 2)],
        dimension_numbers=("NWC", "WIO", "NWC"),
        feature_group_count=ch)
    return y + b


def split_heads(t):
    b, l, _ = t.shape
    return t.reshape(b, l, NA_HEADS, HEAD_DIM).transpose(0, 2, 1, 3)


def merge_heads(t):
    b, h, l, d = t.shape
    return t.transpose(0, 2, 1, 3).reshape(b, l, h * d)


def rope_2d(x, pos_row, pos_col):
    half = HEAD_DIM // 2
    inv_freq = ROPE_THETA ** (-jnp.arange(0, half, 2, dtype=jnp.float32) / half)

    def rot(xa, pos):
        ang = pos[:, None] * inv_freq[None, :]
        cos, sin = jnp.cos(ang), jnp.sin(ang)
        x1, x2 = jnp.split(xa.astype(jnp.float32), 2, axis=-1)
        return jnp.concatenate([x1 * cos - x2 * sin, x1 * sin + x2 * cos], axis=-1)

    out = jnp.concatenate([rot(x[..., :half], pos_row), rot(x[..., half:], pos_col)], axis=-1)
    return out.astype(x.dtype)


def conv_branch(p_glu, dw, dw_b, ln_g, ln_b, w_o):
    a, g = jnp.split(p_glu, 2, axis=-1)
    u = a * jax.nn.sigmoid(g)
    u = dwconv1d(u, dw, dw_b)
    u = jax.nn.silu(layernorm(u, ln_g, ln_b))
    return u @ w_o


def neighbourhood_attention(q, k, v, k_ctx, v_ctx, rpb):
    b, h, l, dh = q.shape
    rows = l // GRID_W
    kh = min(NA_KH_MAX, rows)
    n_win = kh * NA_KW
    qg = q.reshape(b, h, rows, GRID_W, dh)
    kg = k.reshape(b, h, rows, GRID_W, dh)
    vg = v.reshape(b, h, rows, GRID_W, dh)
    cols = jnp.arange(GRID_W)
    col_start = jnp.clip(cols - NA_KW // 2, 0, GRID_W - NA_KW)
    col_idx = col_start[:, None] + jnp.arange(NA_KW)[None, :]
    dc_idx = col_idx - cols[:, None] + (NA_KW - 1)
    scale = HEAD_DIM ** -0.5

    def row_block(r):
        row_start = jnp.clip(r - kh // 2, 0, rows - kh)
        q_r = lax.dynamic_index_in_dim(qg, r, axis=2, keepdims=False)
        k_band = lax.dynamic_slice_in_dim(kg, row_start, kh, axis=2)
        v_band = lax.dynamic_slice_in_dim(vg, row_start, kh, axis=2)
        k_win = k_band[:, :, :, col_idx]
        v_win = v_band[:, :, :, col_idx]
        s_win = jnp.einsum("bhqd,bhiqjd->bhqij", q_r, k_win,
                           preferred_element_type=jnp.float32) * scale
        dr_idx = row_start + jnp.arange(kh) - r + (NA_KH_MAX - 1)
        bias = rpb[:, dr_idx[None, :, None], dc_idx[:, None, :]]
        s_win = s_win + bias[None].astype(jnp.float32)
        s_ctx = jnp.einsum("bhqd,bhcd->bhqc", q_r, k_ctx,
                           preferred_element_type=jnp.float32) * scale
        s = jnp.concatenate([s_win.reshape(b, h, GRID_W, n_win), s_ctx], axis=-1)
        p = jax.nn.softmax(s, axis=-1).astype(v.dtype)
        p_win = p[..., :n_win].reshape(b, h, GRID_W, kh, NA_KW)
        p_ctx = p[..., n_win:]
        return (jnp.einsum("bhqij,bhiqjd->bhqd", p_win, v_win)
                + jnp.einsum("bhqc,bhcd->bhqd", p_ctx, v_ctx))

    out = lax.map(row_block, jnp.arange(rows))
    return out.transpose(1, 2, 0, 3, 4).reshape(b, h, l, dh)


def context_attention(q, k, v):
    s = jnp.einsum("bhqd,bhkd->bhqk", q, k, preferred_element_type=jnp.float32) * (HEAD_DIM ** -0.5)
    p = jax.nn.softmax(s, axis=-1).astype(v.dtype)
    return jnp.einsum("bhqk,bhkd->bhqd", p, v)


def gated_merge(p_gate, y_conv, y_attn, w_o):
    g_conv, g_attn = jnp.split(jax.nn.sigmoid(p_gate), 2, axis=-1)
    return (g_conv * y_conv + g_attn * y_attn) @ w_o


def conv_ffn(h, w_up, dw, dw_b, w_down):
    u = dwconv1d(h @ w_up, dw, dw_b)
    a, g = jnp.split(u, 2, axis=-1)
    return (jax.nn.silu(g) * a) @ w_down


def setup_inputs(seed: int = 0) -> dict:
    key = jax.random.key(seed)
    ks = jax.random.split(key, 24)
    n = jax.random.normal
    f32 = jnp.float32
    D = D_MODEL
    return {
        "x": n(ks[0], (BATCH, SEQ, D), f32),
        "c": n(ks[1], (BATCH, D), f32),
        "ctx": n(ks[2], (BATCH, CTX_LEN, D), f32),
        "c_ctx": n(ks[3], (D,), f32),
        "w_ada": n(ks[4], (DEPTH, D, 6 * D), f32) * (0.5 * D ** -0.5),
        "b_ada": n(ks[5], (DEPTH, 6 * D), f32) * 0.02,
        "norm1_g": 1.0 + 0.02 * n(ks[6], (DEPTH, D), f32),
        "w_in": n(ks[7], (DEPTH, D, IN_DIM), f32) * D ** -0.5,
        "conv_dw": n(ks[8], (DEPTH, CONV_K, CONV_DIM), f32) * CONV_K ** -0.5,
        "conv_dw_b": n(ks[9], (DEPTH, CONV_DIM), f32) * 0.02,
        "conv_ln_g": 1.0 + 0.02 * n(ks[10], (DEPTH, CONV_DIM), f32),
        "conv_ln_b": n(ks[11], (DEPTH, CONV_DIM), f32) * 0.02,
        "w_conv_out": n(ks[12], (DEPTH, CONV_DIM, D), f32) * CONV_DIM ** -0.5,
        "na_rpb": n(ks[13], (DEPTH, NA_HEADS, 2 * NA_KH_MAX - 1, 2 * NA_KW - 1), f32) * 0.1,
        "w_na_out": n(ks[14], (DEPTH, NA_DIM, D), f32) * NA_DIM ** -0.5,
        "w_out": n(ks[15], (DEPTH, D, D), f32) * D ** -0.5,
        "norm2_g": 1.0 + 0.02 * n(ks[16], (DEPTH, D), f32),
        "w_up": n(ks[17], (DEPTH, D, 2 * FFN_DIM), f32) * D ** -0.5,
        "ffn_dw": n(ks[18], (DEPTH, FFN_CONV_K, 2 * FFN_DIM), f32) * FFN_CONV_K ** -0.5,
        "ffn_dw_b": n(ks[19], (DEPTH, 2 * FFN_DIM), f32) * 0.02,
        "w_down": n(ks[20], (DEPTH, FFN_DIM, D), f32) * FFN_DIM ** -0.5,
        "final_norm_g": 1.0 + 0.02 * n(ks[21], (D,), f32),
    }


def reference(x, c, ctx, c_ctx, w_ada, b_ada, norm1_g, w_in, conv_dw, conv_dw_b, conv_ln_g,
              conv_ln_b, w_conv_out, na_rpb, w_na_out, w_out, norm2_g, w_up, ffn_dw, ffn_dw_b,
              w_down, final_norm_g):
    L = x.shape[1]
    t = jnp.arange(L)
    pos_row = (t // GRID_W).astype(jnp.float32)
    pos_col = (t % GRID_W).astype(jnp.float32)
    xc = ctx
    s_lat = jax.nn.silu(c)
    s_ctx = jax.nn.silu(c_ctx)
    for l in range(DEPTH):
        last = l == DEPTH - 1
        mod = (s_lat @ w_ada[l] + b_ada[l])[:, None, :]
        sh1, sc1, g1, sh2, sc2, g2 = jnp.split(mod, 6, axis=-1)
        mod_c = s_ctx @ w_ada[l] + b_ada[l]
        csh1, csc1, cg1, csh2, csc2, cg2 = jnp.split(mod_c, 6, axis=-1)

        h = modulate(rmsnorm(x, norm1_g[l]), sh1, sc1)
        hc = modulate(rmsnorm(xc, norm1_g[l]), csh1, csc1)
        p = h @ w_in[l]
        if last:
            kv_c = hc @ w_in[l][:, GLU_END + NA_DIM:QKV_END]
            k_c, v_c = (split_heads(u) for u in jnp.split(kv_c, 2, axis=-1))
        else:
            pc = hc @ w_in[l]
            q_c, k_c, v_c = (split_heads(u) for u in jnp.split(pc[..., GLU_END:QKV_END], 3, axis=-1))

        q, k, v = (split_heads(u) for u in jnp.split(p[..., GLU_END:QKV_END], 3, axis=-1))
        q = rope_2d(q, pos_row, pos_col)
        k = rope_2d(k, pos_row, pos_col)
        y_conv = conv_branch(p[..., :GLU_END], conv_dw[l], conv_dw_b[l], conv_ln_g[l],
                             conv_ln_b[l], w_conv_out[l])
        y_attn = merge_heads(neighbourhood_attention(q, k, v, k_c, v_c, na_rpb[l])) @ w_na_out[l]
        x = x + g1 * gated_merge(p[..., QKV_END:], y_conv, y_attn, w_out[l])

        h2 = modulate(rmsnorm(x, norm2_g[l]), sh2, sc2)
        x = x + g2 * conv_ffn(h2, w_up[l], ffn_dw[l], ffn_dw_b[l], w_down[l])

        if not last:
            yc_conv = conv_branch(pc[..., :GLU_END], conv_dw[l], conv_dw_b[l], conv_ln_g[l],
                                  conv_ln_b[l], w_conv_out[l])
            yc_attn = merge_heads(context_attention(q_c, k_c, v_c)) @ w_na_out[l]
            xc = xc + cg1 * gated_merge(pc[..., QKV_END:], yc_conv, yc_attn, w_out[l])
            hc2 = modulate(rmsnorm(xc, norm2_g[l]), csh2, csc2)
            xc = xc + cg2 * conv_ffn(hc2, w_up[l], ffn_dw[l], ffn_dw_b[l], w_down[l])
    return rmsnorm(x, final_norm_g)
```

```python
import functools

import jax
import jax.numpy as jnp
from jax import lax
from jax.experimental import pallas as pl
from jax.experimental.pallas import tpu as pltpu

D_MODEL = 1024
GRID_W = 64
CONV_DIM = 512
CONV_K = 31
NA_HEADS = 8
HEAD_DIM = 64
NA_DIM = NA_HEADS * HEAD_DIM
NA_KH = 8
NA_KW = 16
ROPE_THETA = 10000.0
FFN_DIM = 2816
FFN_CONV_K = 3
EPS = 1e-6
GLU_END = 2 * CONV_DIM
QKV_END = GLU_END + 3 * NA_DIM

LANES = 128
SUBLANES = 8
BF16_ROWS = 16
V7X_VMEM_BYTES = 64 * 1024 * 1024
VMEM_LIMIT = V7X_VMEM_BYTES * 7 // 8

HEADS_PER_BLOCK = LANES // HEAD_DIM
N_HEAD_BLOCKS = NA_HEADS // HEADS_PER_BLOCK
CONV_HALO = BF16_ROWS
FFN_HALO = SUBLANES
FFN_CHUNK = 256
N_FFN_CHUNKS = FFN_DIM // FFN_CHUNK
MOD_ROWS = 8

BF16 = jnp.bfloat16
F32 = jnp.float32


def _params(semantics):
    return pltpu.CompilerParams(dimension_semantics=semantics, vmem_limit_bytes=VMEM_LIMIT)


def _const_spec(shape):
    zeros = (0,) * len(shape)
    return pl.BlockSpec(shape, lambda *_: zeros, pipeline_mode=pl.Buffered(1))


def _rms_modulate(x, gain, shift, scale):
    y = x * lax.rsqrt(jnp.mean(x * x, axis=-1, keepdims=True) + EPS) * gain
    return y * (1.0 + scale) + shift


def _ada_body(c_ref, w_ref, b_ref, o_ref):
    s = jax.nn.silu(c_ref[...]).astype(BF16)
    o_ref[0] = jnp.dot(s, w_ref[0].astype(BF16), preferred_element_type=F32) + b_ref[0]


def _ada(cvec, w_ada, b_ada):
    depth, d, n = w_ada.shape
    tn = 512
    return pl.pallas_call(
        _ada_body,
        grid=(depth, n // tn),
        in_specs=[
            pl.BlockSpec((MOD_ROWS, d), lambda l, j: (0, 0)),
            pl.BlockSpec((1, d, tn), lambda l, j: (l, 0, j)),
            pl.BlockSpec((1, 1, tn), lambda l, j: (l, 0, j)),
        ],
        out_specs=pl.BlockSpec((1, MOD_ROWS, tn), lambda l, j: (l, 0, j)),
        out_shape=jax.ShapeDtypeStruct((depth, MOD_ROWS, n), F32),
        compiler_params=_params(("arbitrary", "arbitrary")),
        name="ada",
    )(cvec, w_ada, b_ada.reshape(depth, 1, n))


def _bias_body(rpb_ref, o_ref):
    hb = pl.program_id(0)
    c = lax.broadcasted_iota(jnp.int32, (GRID_W, GRID_W), 0)
    cp = lax.broadcasted_iota(jnp.int32, (GRID_W, GRID_W), 1)
    start = jnp.clip(c - NA_KW // 2, 0, GRID_W - NA_KW)
    inside = (cp >= start) & (cp < start + NA_KW)
    diff = cp - c + (NA_KW - 1)
    n_dr = 2 * NA_KH - 1
    n_dc = 2 * NA_KW - 1
    for hh in range(HEADS_PER_BLOCK):
        h = hb * HEADS_PER_BLOCK + hh
        for i in range(n_dr):
            val = jnp.zeros((GRID_W, GRID_W), F32)
            for d in range(n_dc):
                val = jnp.where(diff == d, rpb_ref[(h * n_dr + i) * n_dc + d], val)
            val = jnp.where(inside, val, -jnp.inf)
            for v in range(NA_KH):
                j = i - v
                if 0 <= j < NA_KH:
                    o_ref[v, 0, hh * GRID_W:(hh + 1) * GRID_W, j * GRID_W:(j + 1) * GRID_W] = val


def _window_bias(rpb):
    return pl.pallas_call(
        _bias_body,
        grid=(N_HEAD_BLOCKS,),
        in_specs=[pl.BlockSpec(memory_space=pltpu.SMEM)],
        out_specs=pl.BlockSpec((NA_KH, 1, HEADS_PER_BLOCK * GRID_W, NA_KH * GRID_W), lambda hb: (0, hb, 0, 0)),
        out_shape=jax.ShapeDtypeStruct((NA_KH, N_HEAD_BLOCKS, HEADS_PER_BLOCK * GRID_W, NA_KH * GRID_W), F32),
        compiler_params=_params(("arbitrary",)),
        name="window_bias",
    )(rpb.reshape(-1))


def _rope(t, cos, sin, swap_up):
    partner = jnp.where(swap_up, pltpu.roll(t, LANES - HEAD_DIM // 4, 1), pltpu.roll(t, HEAD_DIM // 4, 1))
    return t * cos + partner * sin


def _inproj_body(x_ref, mod_ref, g_ref, cos_ref, sin_ref, wglu_ref, wqkv_ref, u_ref, q_ref, k_ref, v_ref):
    mod = mod_ref[0]
    h = _rms_modulate(x_ref[0], g_ref[...], mod[0:1], mod[1:2]).astype(BF16)
    pg = jnp.dot(h, wglu_ref[...], preferred_element_type=F32)
    u_ref[0] = (pg[:, :CONV_DIM] * jax.nn.sigmoid(pg[:, CONV_DIM:])).astype(BF16)
    pq = jnp.dot(h, wqkv_ref[...], preferred_element_type=F32)
    cos = cos_ref[...]
    sin = sin_ref[...]
    lane = lax.broadcasted_iota(jnp.int32, cos.shape, 1)
    swap_up = (lane % (HEAD_DIM // 2)) < (HEAD_DIM // 4)
    for blk in range(N_HEAD_BLOCKS):
        sl = slice(blk * LANES, (blk + 1) * LANES)
        q_ref[0, :, sl] = (_rope(pq[:, sl], cos, sin, swap_up) * (HEAD_DIM ** -0.5)).astype(BF16)
        ks = slice(NA_DIM + blk * LANES, NA_DIM + (blk + 1) * LANES)
        k_ref[0, :, sl] = _rope(pq[:, ks], cos, sin, swap_up).astype(BF16)
    v_ref[0] = pq[:, 2 * NA_DIM:].astype(BF16)


def _inproj(x, mods, mod_row, gain, cos, sin, w_glu, w_qkv, tm):
    b, t, d = x.shape
    seq_spec = lambda width: pl.BlockSpec((1, tm, width), lambda bi, i: (bi, i, 0))
    out = jax.ShapeDtypeStruct((b, t, NA_DIM), BF16)
    return pl.pallas_call(
        _inproj_body,
        grid=(b, t // tm),
        in_specs=[
            seq_spec(d),
            pl.BlockSpec((1, 6, d), lambda bi, i: (mod_row(bi), 0, 0)),
            _const_spec((1, d)),
            pl.BlockSpec((tm, LANES), lambda bi, i: (i, 0)),
            pl.BlockSpec((tm, LANES), lambda bi, i: (i, 0)),
            _const_spec(w_glu.shape),
            _const_spec(w_qkv.shape),
        ],
        out_specs=[seq_spec(CONV_DIM), seq_spec(NA_DIM), seq_spec(NA_DIM), seq_spec(NA_DIM)],
        out_shape=[jax.ShapeDtypeStruct((b, t, CONV_DIM), BF16), out, out, out],
        compiler_params=_params(("arbitrary", "arbitrary")),
        name="inproj",
    )(x, mods, gain, cos, sin, w_glu, w_qkv)


def _stack_heads(t):
    lane = lax.broadcasted_iota(jnp.int32, t.shape, 1)
    zero = jnp.zeros_like(t)
    return jnp.concatenate([jnp.where(lane < HEAD_DIM, t, zero), jnp.where(lane >= HEAD_DIM, t, zero)], axis=0)


def _unstack_heads(o, rows):
    lane = lax.broadcasted_iota(jnp.int32, (rows, LANES), 1)
    return jnp.where(lane < HEAD_DIM, o[:rows], o[rows:])


def _scores(lhs, keys):
    return lax.dot_general(lhs, keys, (((1,), (1,)), ((), ())), preferred_element_type=F32)


def _na_body(q_ref, k_ref, v_ref, kc_ref, vc_ref, bias_ref, o_ref):
    rows = q_ref.shape[1] // GRID_W
    band = NA_KH * GRID_W
    kc = kc_ref[0]
    vc = vc_ref[0]

    def row(r, carry):
        row_start = jnp.clip(r - NA_KH // 2, 0, rows - NA_KH)
        variant = row_start - r + (NA_KH - 1)
        q0 = pl.multiple_of(r * GRID_W, GRID_W)
        k0 = pl.multiple_of(row_start * GRID_W, GRID_W)
        lhs = _stack_heads(q_ref[0, pl.ds(q0, GRID_W), :])
        s_win = _scores(lhs, k_ref[0, pl.ds(k0, band), :]) + bias_ref[variant, 0]
        s_ctx = _scores(lhs, kc)
        m = jnp.maximum(jnp.max(s_win, axis=-1, keepdims=True), jnp.max(s_ctx, axis=-1, keepdims=True))
        p_win = jnp.exp(s_win - m)
        p_ctx = jnp.exp(s_ctx - m)
        denom = jnp.sum(p_win, axis=-1, keepdims=True) + jnp.sum(p_ctx, axis=-1, keepdims=True)
        o = jnp.dot(p_win.astype(BF16), v_ref[0, pl.ds(k0, band), :], preferred_element_type=F32)
        o = o + jnp.dot(p_ctx.astype(BF16), vc, preferred_element_type=F32)
        o_ref[0, pl.ds(q0, GRID_W), :] = _unstack_heads(o / denom, GRID_W).astype(BF16)
        return carry

    lax.fori_loop(0, rows, row, 0)


def _neighbourhood_attention(q, k, v, kc, vc, bias):
    b, t, _ = q.shape
    ctx = kc.shape[1]
    seq = lambda length: pl.BlockSpec((1, length, LANES), lambda bi, hb: (bi, 0, hb))
    return pl.pallas_call(
        _na_body,
        grid=(b, N_HEAD_BLOCKS),
        in_specs=[seq(t), seq(t), seq(t), seq(ctx), seq(ctx),
                  pl.BlockSpec((NA_KH, 1) + bias.shape[2:], lambda bi, hb: (0, hb, 0, 0))],
        out_specs=seq(t),
        out_shape=jax.ShapeDtypeStruct((b, t, NA_DIM), BF16),
        compiler_params=_params(("arbitrary", "arbitrary")),
        name="na_attention",
    )(q, k, v, kc, vc, bias)


def _ctx_attn_body(q_ref, k_ref, v_ref, o_ref):
    n = q_ref.shape[1]
    lhs = _stack_heads(q_ref[0])
    s = _scores(lhs, k_ref[0])
    p = jnp.exp(s - jnp.max(s, axis=-1, keepdims=True))
    denom = jnp.sum(p, axis=-1, keepdims=True)
    o = jnp.dot(p.astype(BF16), v_ref[0], preferred_element_type=F32)
    o_ref[0] = _unstack_heads(o / denom, n).astype(BF16)


def _context_attention(q, k, v):
    b, n, _ = q.shape
    spec = pl.BlockSpec((1, n, LANES), lambda bi, hb: (bi, 0, hb))
    return pl.pallas_call(
        _ctx_attn_body,
        grid=(b, N_HEAD_BLOCKS),
        in_specs=[spec, spec, spec],
        out_specs=spec,
        out_shape=jax.ShapeDtypeStruct((b, n, NA_DIM), BF16),
        compiler_params=_params(("arbitrary", "arbitrary")),
        name="ctx_attention",
    )(q, k, v)


def _mix_body(x_ref, up_ref, uc_ref, un_ref, a_ref, mod_ref, g_ref, dw_ref, dwb_ref, lng_ref, lnb_ref,
              wgate_ref, wconv_ref, wna_ref, wout_ref, o_ref, ucat_ref):
    i = pl.program_id(1)
    tm = x_ref.shape[1]
    prev = up_ref[0].astype(F32)
    nxt = un_ref[0].astype(F32)
    ucat_ref[0:CONV_HALO] = jnp.where(i > 0, prev, jnp.zeros_like(prev))
    ucat_ref[CONV_HALO:CONV_HALO + tm] = uc_ref[0].astype(F32)
    ucat_ref[CONV_HALO + tm:] = jnp.where(i < pl.num_programs(1) - 1, nxt, jnp.zeros_like(nxt))
    acc = jnp.zeros((tm, CONV_DIM), F32) + dwb_ref[...]
    first = CONV_HALO - CONV_K // 2
    for tap in range(CONV_K):
        acc = acc + dw_ref[tap:tap + 1, :] * ucat_ref[first + tap:first + tap + tm, :]
    mu = jnp.mean(acc, axis=-1, keepdims=True)
    cen = acc - mu
    var = jnp.mean(cen * cen, axis=-1, keepdims=True)
    feat = jax.nn.silu(cen * lax.rsqrt(var + EPS) * lng_ref[...] + lnb_ref[...]).astype(BF16)
    y_conv = jnp.dot(feat, wconv_ref[...], preferred_element_type=F32)
    y_attn = jnp.dot(a_ref[0], wna_ref[...], preferred_element_type=F32)
    x = x_ref[0]
    mod = mod_ref[0]
    h = _rms_modulate(x, g_ref[...], mod[0:1], mod[1:2]).astype(BF16)
    gates =jax.nn.sigmoid(jnp.dot(h, wgate_ref[...], preferred_element_type=F32))
    merged = (gates[:, :D_MODEL] * y_conv + gates[:, D_MODEL:] * y_attn).astype(BF16)
    o_ref[0] = x + mod[2:3] * jnp.dot(merged, wout_ref[...], preferred_element_type=F32)


def _mix(x, u, attn, mods, mod_row, gain, dw, dwb, lng, lnb, w_gate, w_conv, w_na, w_out, tm):
    b, t, d = x.shape
    per = tm // CONV_HALO
    last = t // CONV_HALO - 1
    seq = lambda width: pl.BlockSpec((1, tm, width), lambda bi, i: (bi, i, 0))
    return pl.pallas_call(
        _mix_body,
        grid=(b, t // tm),
        in_specs=[
            seq(d),
            pl.BlockSpec((1, CONV_HALO, CONV_DIM), lambda bi, i: (bi, jnp.maximum(i * per - 1, 0), 0)),
            seq(CONV_DIM),
            pl.BlockSpec((1, CONV_HALO, CONV_DIM), lambda bi, i: (bi, jnp.minimum((i + 1) * per, last), 0)),
            seq(NA_DIM),
            pl.BlockSpec((1, 6, d), lambda bi, i: (mod_row(bi), 0, 0)),
            _const_spec((1, d)),
            _const_spec(dw.shape), _const_spec(dwb.shape), _const_spec(lng.shape), _const_spec(lnb.shape),
            _const_spec(w_gate.shape), _const_spec(w_conv.shape), _const_spec(w_na.shape), _const_spec(w_out.shape),
        ],
        out_specs=seq(d),
        out_shape=jax.ShapeDtypeStruct((b, t, d), F32),
        scratch_shapes=[pltpu.VMEM((tm + 2 * CONV_HALO, CONV_DIM), F32)],
        compiler_params=_params(("arbitrary", "arbitrary")),
        name="mix",
    )(x, u, u, u, attn, mods, gain, dw, dwb, lng, lnb, w_gate, w_conv, w_na, w_out)


def _ffn_body(final_norm, x_ref, xp_ref, xn_ref, mod_ref, g_ref, wup_ref, dw_ref, dwb_ref, wdown_ref, fg_ref,
              o_ref, acc_ref):
    i = pl.program_id(1)
    tm = x_ref.shape[1]
    mod = mod_ref[0]
    gain = g_ref[...]
    x = x_ref[0]
    hp = _rms_modulate(xp_ref[0], gain, mod[3:4], mod[4:5])
    hn = _rms_modulate(xn_ref[0], gain, mod[3:4], mod[4:5])
    hp = jnp.where(i > 0, hp, jnp.zeros_like(hp))
    hn = jnp.where(i < pl.num_programs(1) - 1, hn, jnp.zeros_like(hn))
    h = jnp.concatenate([hp, _rms_modulate(x, gain, mod[3:4], mod[4:5]), hn], axis=0).astype(BF16)
    lo =FFN_HALO - FFN_CONV_K // 2

    def conv(c):
        up = jnp.dot(h, wup_ref[c], preferred_element_type=F32)
        out = dwb_ref[c]
        for tap in range(FFN_CONV_K):
            out = out + dw_ref[c, tap:tap + 1, :] * up[lo + tap:lo + tap + tm, :]
        return out

    for c in range(N_FFN_CHUNKS):
        act = (jax.nn.silu(conv(N_FFN_CHUNKS + c)) * conv(c)).astype(BF16)
        part = jnp.dot(act, wdown_ref[c], preferred_element_type=F32)
        if c == 0:
            acc_ref[...] = part
        else:
            acc_ref[...] += part
    y = x + mod[5:6] * acc_ref[...]
    if final_norm:
        y = y * lax.rsqrt(jnp.mean(y * y, axis=-1, keepdims=True) + EPS) * fg_ref[...]
    o_ref[0] = y


def _ffn(x, mods, mod_row, gain, w_up, dw, dwb, w_down, final_gain, final_norm, tm):
    b, t, d = x.shape
    per = tm // FFN_HALO
    last = t // FFN_HALO - 1
    seq = pl.BlockSpec((1, tm, d), lambda bi, i: (bi, i, 0))
    return pl.pallas_call(
        functools.partial(_ffn_body, final_norm),
        grid=(b, t // tm),
        in_specs=[
            seq,
            pl.BlockSpec((1, FFN_HALO, d), lambda bi, i: (bi, jnp.maximum(i * per - 1, 0), 0)),
            pl.BlockSpec((1, FFN_HALO, d), lambda bi, i: (bi, jnp.minimum((i + 1) * per, last), 0)),
            pl.BlockSpec((1, 6, d), lambda bi, i: (mod_row(bi), 0, 0)),
            _const_spec((1, d)),
            _const_spec(w_up.shape), _const_spec(dw.shape), _const_spec(dwb.shape), _const_spec(w_down.shape),
            _const_spec((1, d)),
        ],
        out_specs=seq,
        out_shape=jax.ShapeDtypeStruct((b, t, d), F32),
        scratch_shapes=[pltpu.VMEM((tm, d), F32)],
        compiler_params=_params(("arbitrary", "arbitrary")),
        name="ffn",
    )(x, x, x, mods, gain, w_up, dw, dwb, w_down, final_gain)


def _rope_tables(seq_len):
    half = HEAD_DIM // 2
    inv_freq = ROPE_THETA ** (-jnp.arange(0, half, 2, dtype=F32) / half)
    t = jnp.arange(seq_len)
    ang_r = (t // GRID_W).astype(F32)[:, None] * inv_freq[None, :]
    ang_c = (t % GRID_W).astype(F32)[:, None] * inv_freq[None, :]
    cos = jnp.concatenate([jnp.cos(ang_r)] * 2 + [jnp.cos(ang_c)] * 2, axis=-1)
    sin = jnp.concatenate([-jnp.sin(ang_r), jnp.sin(ang_r), -jnp.sin(ang_c), jnp.sin(ang_c)], axis=-1)
    return jnp.tile(cos, (1, HEADS_PER_BLOCK)), jnp.tile(sin, (1, HEADS_PER_BLOCK))


def _chunked_up(w, n_chunks):
    lead = w.shape[:-1]
    w = w.reshape(lead + (2 * n_chunks, FFN_CHUNK))
    return jnp.moveaxis(w, -2, 0)


def kernel(x, c, ctx, c_ctx, w_ada, b_ada, norm1_g, w_in, conv_dw, conv_dw_b, conv_ln_g, conv_ln_b, w_conv_out,
           na_rpb, w_na_out, w_out, norm2_g, w_up, ffn_dw, ffn_dw_b, w_down, final_norm_g):
    batch, seq_len, d = x.shape
    depth = w_ada.shape[0]
    ctx_len = ctx.shape[1]
    assert d == D_MODEL and seq_len % GRID_W == 0 and batch + 1 <= MOD_ROWS

    cvec = jnp.zeros((MOD_ROWS, d), F32).at[:batch].set(c).at[batch].set(c_ctx)
    mods = _ada(cvec, w_ada, b_ada).reshape(depth, MOD_ROWS, 6, d)
    lat_row = lambda bi: bi
    ctx_row = lambda bi: batch

    cos, sin = _rope_tables(seq_len)
    cos_id = jnp.ones((ctx_len, LANES), F32)
    sin_id = jnp.zeros((ctx_len, LANES), F32)
    row = lambda v: v.reshape(1, -1)

    xc = ctx
    for l in range(depth):
        last = l == depth - 1
        w_glu = w_in[l, :, :GLU_END].astype(BF16)
        w_qkv = w_in[l, :, GLU_END:QKV_END].astype(BF16)
        w_gate = w_in[l, :, QKV_END:].astype(BF16)
        w_conv = w_conv_out[l].astype(BF16)
        w_na = w_na_out[l].astype(BF16)
        w_o = w_out[l].astype(BF16)
        w_u = _chunked_up(w_up[l], N_FFN_CHUNKS).astype(BF16)
        f_dw = _chunked_up(ffn_dw[l], N_FFN_CHUNKS)
        f_dwb = _chunked_up(ffn_dw_b[l].reshape(1, -1), N_FFN_CHUNKS)
        w_d = w_down[l].reshape(N_FFN_CHUNKS, FFN_CHUNK, d).astype(BF16)
        g1 = row(norm1_g[l])
        g2 = row(norm2_g[l])
        conv_args = (conv_dw[l], row(conv_dw_b[l]), row(conv_ln_g[l]), row(conv_ln_b[l]))
        fin = row(final_norm_g)
        bias = _window_bias(na_rpb[l])

        u_c, q_c, k_c, v_c = _inproj(xc, mods[l], ctx_row, g1, cos_id, sin_id, w_glu, w_qkv, ctx_len)
        u, q, k, v = _inproj(x, mods[l], lat_row, g1, cos, sin, w_glu, w_qkv, 512)
        attn = _neighbourhood_attention(q, k, v, k_c, v_c, bias)
        x = _mix(x, u, attn, mods[l], lat_row, g1, *conv_args, w_gate, w_conv, w_na, w_o, 256)
        x = _ffn(x, mods[l], lat_row, g2, w_u, f_dw, f_dwb, w_d, fin, last, 512)
        if not last:
            attn_c = _context_attention(q_c, k_c, v_c)
            xc = _mix(xc, u_c, attn_c, mods[l], ctx_row, g1, *conv_args, w_gate, w_conv, w_na, w_o, ctx_len)
            xc = _ffn(xc, mods[l], ctx_row, g2, w_u, f_dw, f_dwb, w_d, fin, False, ctx_len)
    return x
```

```python
import functools

import jax
import jax.numpy as jnp
from jax import lax
from jax.experimental import pallas as pl
from jax.experimental.pallas import tpu as pltpu

D_MODEL = 1024
GRID_W = 64
CONV_DIM = 512
CONV_K = 31
NA_HEADS = 8
HEAD_DIM = 64
NA_DIM = NA_HEADS * HEAD_DIM
NA_KH = 8
NA_KW = 16
ROPE_THETA = 10000.0
FFN_DIM = 2816
FFN_CONV_K = 3
EPS = 1e-6
GLU_END = 2 * CONV_DIM
QKV_END = GLU_END + 3 * NA_DIM

LANES = 128
SUBLANES = 8
BF16_ROWS = 16
V7X_VMEM_BYTES = 64 * 1024 * 1024
VMEM_LIMIT = V7X_VMEM_BYTES * 7 // 8

HEADS_PER_BLOCK = LANES // HEAD_DIM
N_HEAD_BLOCKS = NA_HEADS // HEADS_PER_BLOCK
CONV_HALO = BF16_ROWS
FFN_HALO = SUBLANES
FFN_CHUNK = 256
N_FFN_CHUNKS = FFN_DIM // FFN_CHUNK
MOD_ROWS = 8
N_MOD = 6
NA_ROW_GROUP = 8

INPROJ_TM = 512
MIX_TM = 256
FFN_TM = 512

BF16 = jnp.bfloat16
F32 = jnp.float32


def _params(semantics):
    return pltpu.CompilerParams(dimension_semantics=semantics, vmem_limit_bytes=VMEM_LIMIT)


def _layer_spec(arr, layer):
    tail = (0,) * (arr.ndim - 1)
    return pl.BlockSpec((1,) + arr.shape[1:], lambda *_: (layer,) + tail, pipeline_mode=pl.Buffered(1))


def _mod_spec(layer, mod_row):
    return pl.BlockSpec((1, 1, N_MOD, D_MODEL), lambda bi, i: (layer, mod_row(bi), 0, 0))


def _rms_modulate(x, gain, shift, scale):
    y = x * lax.rsqrt(jnp.mean(x * x, axis=-1, keepdims=True) + EPS) * gain
    return y * (1.0 + scale) + shift


def _ada_body(c_ref, w_ref, b_ref, o_ref):
    s = jax.nn.silu(c_ref[...]).astype(BF16)
    o_ref[0] = jnp.dot(s, w_ref[0].astype(BF16), preferred_element_type=F32) + b_ref[0]


def _ada(cvec, w_ada, b_ada):
    depth, d, n = w_ada.shape
    tn = 512
    return pl.pallas_call(
        _ada_body,
        grid=(depth, n // tn),
        in_specs=[
            pl.BlockSpec((MOD_ROWS, d), lambda l, j: (0, 0)),
            pl.BlockSpec((1, d, tn), lambda l, j: (l, 0, j)),
            pl.BlockSpec((1, 1, tn), lambda l, j: (l, 0, j)),
        ],
        out_specs=pl.BlockSpec((1, MOD_ROWS, tn), lambda l, j: (l, 0, j)),
        out_shape=jax.ShapeDtypeStruct((depth, MOD_ROWS, n), F32),
        compiler_params=_params(("arbitrary", "arbitrary")),
        name="ada",
    )(cvec, w_ada, b_ada.reshape(depth, 1, n))


def _bias_body(rpb_ref, o_ref):
    layer = pl.program_id(0)
    hb = pl.program_id(1)
    c = lax.broadcasted_iota(jnp.int32, (GRID_W, GRID_W), 0)
    cp = lax.broadcasted_iota(jnp.int32, (GRID_W, GRID_W), 1)
    start = jnp.clip(c - NA_KW // 2, 0, GRID_W - NA_KW)
    inside = (cp >= start) & (cp < start + NA_KW)
    diff = cp - c + (NA_KW - 1)
    n_dr = 2 * NA_KH - 1
    n_dc = 2 * NA_KW - 1
    for hh in range(HEADS_PER_BLOCK):
        h = (layer * N_HEAD_BLOCKS + hb) * HEADS_PER_BLOCK + hh
        for i in range(n_dr):
            val = jnp.zeros((GRID_W, GRID_W), F32)
            for d in range(n_dc):
                val = jnp.where(diff == d, rpb_ref[(h * n_dr + i) * n_dc + d], val)
            val = jnp.where(inside, val, -jnp.inf)
            for v in range(NA_KH):
                j = i - v
                if 0 <= j < NA_KH:
                    o_ref[0, v, 0, hh * GRID_W:(hh + 1) * GRID_W, j * GRID_W:(j + 1) * GRID_W] = val


def _window_bias(rpb):
    depth = rpb.shape[0]
    tile = (HEADS_PER_BLOCK * GRID_W, NA_KH * GRID_W)
    return pl.pallas_call(
        _bias_body,
        grid=(depth, N_HEAD_BLOCKS),
        in_specs=[pl.BlockSpec(memory_space=pltpu.SMEM)],
        out_specs=pl.BlockSpec((1, NA_KH, 1) + tile, lambda l, hb: (l, 0, hb, 0, 0)),
        out_shape=jax.ShapeDtypeStruct((depth, NA_KH, N_HEAD_BLOCKS) + tile, F32),
        compiler_params=_params(("arbitrary", "arbitrary")),
        name="window_bias",
    )(rpb.reshape(-1))


def _rope(t, cos, sin, swap_up):
    partner = jnp.where(swap_up, pltpu.roll(t, LANES - HEAD_DIM // 4, 1), pltpu.roll(t, HEAD_DIM // 4, 1))
    return t * cos + partner * sin


def _inproj_body(x_ref, mod_ref, g_ref, cos_ref, sin_ref, win_ref, u_ref, q_ref, k_ref, v_ref):
    mod = mod_ref[0, 0]
    h = _rms_modulate(x_ref[0], g_ref[0], mod[0:1], mod[1:2]).astype(BF16)
    pg = jnp.dot(h, win_ref[0, :, :GLU_END], preferred_element_type=F32)
    u_ref[0] = (pg[:, :CONV_DIM] * jax.nn.sigmoid(pg[:, CONV_DIM:])).astype(BF16)
    pq = jnp.dot(h, win_ref[0, :, GLU_END:QKV_END], preferred_element_type=F32)
    cos = cos_ref[...]
    sin = sin_ref[...]
    lane = lax.broadcasted_iota(jnp.int32, cos.shape, 1)
    swap_up = (lane % (HEAD_DIM // 2)) < (HEAD_DIM // 4)
    for blk in range(N_HEAD_BLOCKS):
        sl = slice(blk * LANES, (blk + 1) * LANES)
        q_ref[0, :, sl] = (_rope(pq[:, sl], cos, sin, swap_up) * (HEAD_DIM ** -0.5)).astype(BF16)
        ks = slice(NA_DIM + blk * LANES, NA_DIM + (blk + 1) * LANES)
        k_ref[0, :, sl] = _rope(pq[:, ks], cos, sin, swap_up).astype(BF16)
    v_ref[0] = pq[:, 2 * NA_DIM:].astype(BF16)


def _inproj(x, layer, mods, mod_row, gain, cos, sin, w_in, tm):
    b, t, d = x.shape
    seq_spec = lambda width: pl.BlockSpec((1, tm, width), lambda bi, i: (bi, i, 0))
    out = jax.ShapeDtypeStruct((b, t, NA_DIM), BF16)
    return pl.pallas_call(
        _inproj_body,
        grid=(b, t // tm),
        in_specs=[
            seq_spec(d),
            _mod_spec(layer, mod_row),
            _layer_spec(gain, layer),
            pl.BlockSpec((tm, LANES), lambda bi, i: (i, 0)),
            pl.BlockSpec((tm, LANES), lambda bi, i: (i, 0)),
            _layer_spec(w_in, layer),
        ],
        out_specs=[seq_spec(CONV_DIM), seq_spec(NA_DIM), seq_spec(NA_DIM), seq_spec(NA_DIM)],
        out_shape=[jax.ShapeDtypeStruct((b, t, CONV_DIM), BF16), out, out, out],
        compiler_params=_params(("arbitrary", "arbitrary")),
        name="inproj",
    )(x, mods, gain, cos, sin, w_in)


def _stack_heads(t):
    lane = lax.broadcasted_iota(jnp.int32, t.shape, 1)
    zero = jnp.zeros_like(t)
    return jnp.concatenate([jnp.where(lane < HEAD_DIM, t, zero), jnp.where(lane >= HEAD_DIM, t, zero)], axis=0)


def _unstack_heads(o, rows):
    lane = lax.broadcasted_iota(jnp.int32, (rows, LANES), 1)
    return jnp.where(lane < HEAD_DIM, o[:rows], o[rows:])


def _scores(lhs, keys):
    return lax.dot_general(lhs, keys, (((1,), (1,)), ((), ())), preferred_element_type=F32)


def _na_body(q_ref, k_ref, v_ref, kc_ref, vc_ref, bias_ref, o_ref):
    rows = q_ref.shape[1] // GRID_W
    band = NA_KH * GRID_W
    kc = kc_ref[0]
    vc = vc_ref[0]

    def group(g, carry):
        rs = [g * NA_ROW_GROUP + j for j in range(NA_ROW_GROUP)]
        starts = [jnp.clip(r - NA_KH // 2, 0, rows - NA_KH) for r in rs]
        q0 = [pl.multiple_of(r * GRID_W, GRID_W) for r in rs]
        k0 = [pl.multiple_of(s * GRID_W, GRID_W) for s in starts]
        lhs = [_stack_heads(q_ref[0, pl.ds(a, GRID_W), :]) for a in q0]
        s_win = [_scores(l, k_ref[0, pl.ds(a, band), :]) + bias_ref[0, s - r + (NA_KH - 1), 0]
                 for l, a, s, r in zip(lhs, k0, starts, rs)]
        s_ctx = [_scores(l, kc) for l in lhs]
        m = [jnp.maximum(jnp.max(a, axis=-1, keepdims=True), jnp.max(b, axis=-1, keepdims=True))
             for a, b in zip(s_win, s_ctx)]
        p_win = [jnp.exp(a - mm) for a, mm in zip(s_win, m)]
        p_ctx = [jnp.exp(a - mm) for a, mm in zip(s_ctx, m)]
        denom = [jnp.sum(a, axis=-1, keepdims=True) + jnp.sum(b, axis=-1, keepdims=True)
                 for a, b in zip(p_win, p_ctx)]
        for j in range(NA_ROW_GROUP):
            o = jnp.dot(p_win[j].astype(BF16), v_ref[0, pl.ds(k0[j], band), :], preferred_element_type=F32)
            o = o + jnp.dot(p_ctx[j].astype(BF16), vc, preferred_element_type=F32)
            o_ref[0, pl.ds(q0[j], GRID_W), :] = _unstack_heads(o / denom[j], GRID_W).astype(BF16)
        return carry

    lax.fori_loop(0, rows // NA_ROW_GROUP, group, 0)


def _neighbourhood_attention(q, k, v, kc, vc, bias, layer):
    b, t, _ = q.shape
    ctx = kc.shape[1]
    assert (t // GRID_W) % NA_ROW_GROUP == 0
    seq = lambda length: pl.BlockSpec((1, length, LANES), lambda bi, hb: (bi, 0, hb))
    return pl.pallas_call(
        _na_body,
        grid=(b, N_HEAD_BLOCKS),
        in_specs=[seq(t), seq(t), seq(t), seq(ctx), seq(ctx),
                  pl.BlockSpec((1, NA_KH, 1) + bias.shape[3:], lambda bi, hb: (layer, 0, hb, 0, 0))],
        out_specs=seq(t),
        out_shape=jax.ShapeDtypeStruct((b, t, NA_DIM), BF16),
        compiler_params=_params(("arbitrary", "arbitrary")),
        name="na_attention",
    )(q, k, v, kc, vc, bias)


def _ctx_attn_body(q_ref, k_ref, v_ref, o_ref):
    n = q_ref.shape[1]
    lhs = _stack_heads(q_ref[0])
    s = _scores(lhs, k_ref[0])
    p = jnp.exp(s - jnp.max(s, axis=-1, keepdims=True))
    denom = jnp.sum(p, axis=-1, keepdims=True)
    o = jnp.dot(p.astype(BF16), v_ref[0], preferred_element_type=F32)
    o_ref[0] = _unstack_heads(o / denom, n).astype(BF16)


def _context_attention(q, k, v):
    b, n, _ = q.shape
    spec = pl.BlockSpec((1, n, LANES), lambda bi, hb: (bi, 0, hb))
    return pl.pallas_call(
        _ctx_attn_body,
        grid=(b, N_HEAD_BLOCKS),
        in_specs=[spec, spec, spec],
        out_specs=spec,
        out_shape=jax.ShapeDtypeStruct((b, n, NA_DIM), BF16),
        compiler_params=_params(("arbitrary", "arbitrary")),
        name="ctx_attention",
    )(q, k, v)


def _mix_body(x_ref, up_ref, uc_ref, un_ref, a_ref, mod_ref, g_ref, dw_ref, dwb_ref, lng_ref, lnb_ref,
              win_ref, wconv_ref, wna_ref, wout_ref, o_ref):
    i = pl.program_id(1)
    tm = x_ref.shape[1]
    prev = up_ref[0].astype(F32)
    nxt = un_ref[0].astype(F32)
    prev = jnp.where(i > 0, prev, jnp.zeros_like(prev))
    nxt = jnp.where(i < pl.num_programs(1) - 1, nxt, jnp.zeros_like(nxt))
    ucat = jnp.concatenate([prev, uc_ref[0].astype(F32), nxt], axis=0)
    n = tm + 2 * CONV_HALO
    assert CONV_HALO - CONV_K // 2 == 1 and CONV_K < 4 * SUBLANES
    acc = jnp.zeros((tm, CONV_DIM), F32) + dwb_ref[0]
    for b in range(SUBLANES):
        shifted = ucat if b == 0 else pltpu.roll(ucat, n - b, 0)
        for a in range(4):
            j = SUBLANES * a + b
            if 1 <= j <= CONV_K:
                acc = acc + dw_ref[0, j - 1:j, :] * shifted[SUBLANES * a:SUBLANES * a + tm]
    mu = jnp.mean(acc, axis=-1, keepdims=True)
    cen = acc - mu
    var = jnp.mean(cen * cen, axis=-1, keepdims=True)
    feat = jax.nn.silu(cen * lax.rsqrt(var + EPS) * lng_ref[0] + lnb_ref[0]).astype(BF16)
    y_conv = jnp.dot(feat, wconv_ref[0], preferred_element_type=F32)
    y_attn = jnp.dot(a_ref[0], wna_ref[0], preferred_element_type=F32)
    x = x_ref[0]
    mod = mod_ref[0, 0]
    h = _rms_modulate(x, g_ref[0], mod[0:1], mod[1:2]).astype(BF16)
    gates = jax.nn.sigmoid(jnp.dot(h, win_ref[0, :, QKV_END:], preferred_element_type=F32))
    merged = (gates[:, :D_MODEL] * y_conv + gates[:, D_MODEL:] * y_attn).astype(BF16)
    o_ref[0] = x + mod[2:3] * jnp.dot(merged, wout_ref[0], preferred_element_type=F32)


def _mix(x, u, attn, layer, mods, mod_row, p, tm):
    b, t, d = x.shape
    per = tm // CONV_HALO
    last = t // CONV_HALO - 1
    seq = lambda width: pl.BlockSpec((1, tm, width), lambda bi, i: (bi, i, 0))
    weights = (p["norm1_g"], p["conv_dw"], p["conv_dw_b"], p["conv_ln_g"], p["conv_ln_b"],
               p["w_in"], p["w_conv_out"], p["w_na_out"], p["w_out"])
    return pl.pallas_call(
        _mix_body,
        grid=(b, t // tm),
        in_specs=[
            seq(d),
            pl.BlockSpec((1, CONV_HALO, CONV_DIM), lambda bi, i: (bi, jnp.maximum(i * per - 1, 0), 0)),
            seq(CONV_DIM),
            pl.BlockSpec((1, CONV_HALO, CONV_DIM), lambda bi, i: (bi, jnp.minimum((i + 1) * per, last), 0)),
            seq(NA_DIM),
            _mod_spec(layer, mod_row),
        ] + [_layer_spec(w, layer) for w in weights],
        out_specs=seq(d),
        out_shape=jax.ShapeDtypeStruct((b, t, d), F32),
        compiler_params=_params(("arbitrary", "arbitrary")),
        name="mix",
    )(x, u, u, u, attn, mods, *weights)


def _ffn_body(final_norm, x_ref, xp_ref, xn_ref, mod_ref, g_ref, wup_ref, dw_ref, dwb_ref, wdown_ref, fg_ref,
              o_ref, acc_ref):
    i = pl.program_id(1)
    tm = x_ref.shape[1]
    mod = mod_ref[0, 0]
    gain = g_ref[0]
    x = x_ref[0]
    hp = _rms_modulate(xp_ref[0], gain, mod[3:4], mod[4:5])
    hn = _rms_modulate(xn_ref[0], gain, mod[3:4], mod[4:5])
    hp = jnp.where(i > 0, hp, jnp.zeros_like(hp))
    hn = jnp.where(i < pl.num_programs(1) - 1, hn, jnp.zeros_like(hn))
    h = jnp.concatenate([hp, _rms_modulate(x, gain, mod[3:4], mod[4:5]), hn], axis=0).astype(BF16)
    n = tm + 2 * FFN_HALO
    assert FFN_CONV_K == 3

    def cols(c):
        return slice(c * FFN_CHUNK, (c + 1) * FFN_CHUNK)

    def up_proj(c):
        return (jnp.dot(h, wup_ref[0, :, cols(N_FFN_CHUNKS + c)], preferred_element_type=F32),
                jnp.dot(h, wup_ref[0, :, cols(c)], preferred_element_type=F32))

    def conv(c, up):
        before = pltpu.roll(up, 1, 0)[FFN_HALO:FFN_HALO + tm]
        after = pltpu.roll(up, n - 1, 0)[FFN_HALO:FFN_HALO + tm]
        return (dwb_ref[0, :, cols(c)] + dw_ref[0, 0:1, cols(c)] * before
                + dw_ref[0, 1:2, cols(c)] * up[FFN_HALO:FFN_HALO + tm] + dw_ref[0, 2:3, cols(c)] * after)

    ups = up_proj(0)
    for c in range(N_FFN_CHUNKS):
        gate_up, value_up = ups
        if c + 1 < N_FFN_CHUNKS:
            ups = up_proj(c + 1)
        act = (jax.nn.silu(conv(N_FFN_CHUNKS + c, gate_up)) * conv(c, value_up)).astype(BF16)
        part = jnp.dot(act, wdown_ref[0, cols(c), :], preferred_element_type=F32)
        if c == 0:
            acc_ref[...] = part
        else:
            acc_ref[...] += part
    y = x + mod[5:6] * acc_ref[...]
    if final_norm:
        y = y * lax.rsqrt(jnp.mean(y * y, axis=-1, keepdims=True) + EPS) * fg_ref[...]
    o_ref[0] = y


def _ffn(x, layer, mods, mod_row, p, final_gain, final_norm, tm):
    b, t, d = x.shape
    per = tm // FFN_HALO
    last = t // FFN_HALO - 1
    seq = pl.BlockSpec((1, tm, d), lambda bi, i: (bi, i, 0))
    weights = (p["norm2_g"], p["w_up"], p["ffn_dw"], p["ffn_dw_b"], p["w_down"])
    return pl.pallas_call(
        functools.partial(_ffn_body, final_norm),
        grid=(b, t // tm),
        in_specs=[
            seq,
            pl.BlockSpec((1, FFN_HALO, d), lambda bi, i: (bi, jnp.maximum(i * per - 1, 0), 0)),
            pl.BlockSpec((1, FFN_HALO, d), lambda bi, i: (bi, jnp.minimum((i + 1) * per, last), 0)),
            _mod_spec(layer, mod_row),
        ] + [_layer_spec(w, layer) for w in weights] + [
            pl.BlockSpec((1, d), lambda bi, i: (0, 0), pipeline_mode=pl.Buffered(1)),
        ],
        out_specs=seq,
        out_shape=jax.ShapeDtypeStruct((b, t, d), F32),
        scratch_shapes=[pltpu.VMEM((tm, d), F32)],
        compiler_params=_params(("arbitrary", "arbitrary")),
        name="ffn",
    )(x, x, x, mods, *weights, final_gain)


def _rope_tables(seq_len):
    half = HEAD_DIM // 2
    inv_freq = ROPE_THETA ** (-jnp.arange(0, half, 2, dtype=F32) / half)
    t = jnp.arange(seq_len)
    ang_r = (t // GRID_W).astype(F32)[:, None] * inv_freq[None, :]
    ang_c = (t % GRID_W).astype(F32)[:, None] * inv_freq[None, :]
    cos = jnp.concatenate([jnp.cos(ang_r)] * 2 + [jnp.cos(ang_c)] * 2, axis=-1)
    sin = jnp.concatenate([-jnp.sin(ang_r), jnp.sin(ang_r), -jnp.sin(ang_c), jnp.sin(ang_c)], axis=-1)
    return jnp.tile(cos, (1, HEADS_PER_BLOCK)), jnp.tile(sin, (1, HEADS_PER_BLOCK))


def kernel(x, c, ctx, c_ctx, w_ada, b_ada, norm1_g, w_in, conv_dw, conv_dw_b, conv_ln_g, conv_ln_b, w_conv_out,
           na_rpb, w_na_out, w_out, norm2_g, w_up, ffn_dw, ffn_dw_b, w_down, final_norm_g):
    batch, seq_len, d = x.shape
    depth = w_ada.shape[0]
    ctx_len = ctx.shape[1]
    assert d == D_MODEL and seq_len % GRID_W == 0 and batch + 1 <= MOD_ROWS

    cvec = jnp.zeros((MOD_ROWS, d), F32).at[:batch].set(c).at[batch].set(c_ctx)
    mods = _ada(cvec, w_ada, b_ada).reshape(depth, MOD_ROWS, N_MOD, d)
    lat_row = lambda bi: bi
    ctx_row = lambda bi: batch

    cos, sin = _rope_tables(seq_len)
    cos_id = jnp.ones((ctx_len, LANES), F32)
    sin_id = jnp.zeros((ctx_len, LANES), F32)
    rows = lambda v: v.reshape(depth, 1, -1)
    p = {
        "norm1_g": rows(norm1_g), "norm2_g": rows(norm2_g),
        "conv_dw": conv_dw, "conv_dw_b": rows(conv_dw_b), "conv_ln_g": rows(conv_ln_g), "conv_ln_b": rows(conv_ln_b),
        "ffn_dw": ffn_dw, "ffn_dw_b": rows(ffn_dw_b),
        "w_in": w_in.astype(BF16), "w_conv_out": w_conv_out.astype(BF16), "w_na_out": w_na_out.astype(BF16),
        "w_out": w_out.astype(BF16), "w_up": w_up.astype(BF16), "w_down": w_down.astype(BF16),
    }
    fin = final_norm_g.reshape(1, d)
    bias = _window_bias(na_rpb)

    xc = ctx
    for l in range(depth):
        last = l == depth - 1
        u_c, q_c, k_c, v_c = _inproj(xc, l, mods, ctx_row, p["norm1_g"], cos_id, sin_id, p["w_in"], ctx_len)
        u, q, k, v = _inproj(x, l, mods, lat_row, p["norm1_g"], cos, sin, p["w_in"], INPROJ_TM)
        attn = _neighbourhood_attention(q, k, v, k_c, v_c, bias, l)
        x = _mix(x, u, attn, l, mods, lat_row, p, MIX_TM)
        x = _ffn(x, l, mods, lat_row, p, fin, last, FFN_TM)
        if not last:
            attn_c = _context_attention(q_c, k_c, v_c)
            xc = _mix(xc, u_c, attn_c, l, mods, ctx_row, p, ctx_len)
            xc = _ffn(xc, l, mods, ctx_row, p, fin, False, ctx_len)
    return x
```

```python
import functools

import jax
import jax.numpy as jnp
from jax import lax
from jax.experimental import pallas as pl
from jax.experimental.pallas import tpu as pltpu

D_MODEL = 1024
GRID_W = 64
CONV_DIM = 512
CONV_K = 31
NA_HEADS = 8
HEAD_DIM = 64
NA_DIM = NA_HEADS * HEAD_DIM
NA_KH = 8
NA_KW = 16
ROPE_THETA = 10000.0
FFN_DIM = 2816
FFN_CONV_K = 3
EPS = 1e-6
GLU_END = 2 * CONV_DIM
QKV_END = GLU_END + 3 * NA_DIM

LANES = 128
SUBLANES = 8
BF16_ROWS = 16
V7X_VMEM_BYTES = 64 * 1024 * 1024
VMEM_LIMIT = V7X_VMEM_BYTES * 7 // 8

HEADS_PER_BLOCK = LANES // HEAD_DIM
N_HEAD_BLOCKS = NA_HEADS // HEADS_PER_BLOCK
CONV_HALO = BF16_ROWS
FFN_HALO = SUBLANES
MOD_ROWS = 8
N_MOD = 6
NA_ROW_GROUP = 8
NA_SLAB = 32

INPROJ_TM = 512
MIX_TM = 256
FFN_TM = 512

BF16 = jnp.bfloat16
F32 = jnp.float32


def _params(semantics):
    return pltpu.CompilerParams(dimension_semantics=semantics, vmem_limit_bytes=VMEM_LIMIT)


def _layer_spec(arr, layer):
    tail = (0,) * (arr.ndim - 1)
    return pl.BlockSpec((1,) + arr.shape[1:], lambda *_: (layer,) + tail, pipeline_mode=pl.Buffered(1))


def _mod_spec(layer, mod_row):
    return pl.BlockSpec((1, 1, N_MOD, D_MODEL), lambda bi, i: (layer, mod_row(bi), 0, 0))


def _rms_modulate(x, gain, shift, scale):
    y = x * lax.rsqrt(jnp.mean(x * x, axis=-1, keepdims=True) + EPS) * gain
    return y * (1.0 + scale) + shift


def _ada_body(c_ref, w_ref, b_ref, o_ref):
    s = jax.nn.silu(c_ref[...]).astype(BF16)
    o_ref[0] = jnp.dot(s, w_ref[0].astype(BF16), preferred_element_type=F32) + b_ref[0]


def _ada(cvec, w_ada, b_ada):
    depth, d, n = w_ada.shape
    tn = 512
    return pl.pallas_call(
        _ada_body,
        grid=(depth, n // tn),
        in_specs=[
            pl.BlockSpec((MOD_ROWS, d), lambda l, j: (0, 0)),
            pl.BlockSpec((1, d, tn), lambda l, j: (l, 0, j)),
            pl.BlockSpec((1, 1, tn), lambda l, j: (l, 0, j)),
        ],
        out_specs=pl.BlockSpec((1, MOD_ROWS, tn), lambda l, j: (l, 0, j)),
        out_shape=jax.ShapeDtypeStruct((depth, MOD_ROWS, n), F32),
        compiler_params=_params(("arbitrary", "arbitrary")),
        name="ada",
    )(cvec, w_ada, b_ada.reshape(depth, 1, n))


def _bias_body(rpb_ref, o_ref):
    layer = pl.program_id(0)
    hb = pl.program_id(1)
    c = lax.broadcasted_iota(jnp.int32, (GRID_W, GRID_W), 0)
    cp = lax.broadcasted_iota(jnp.int32, (GRID_W, GRID_W), 1)
    start = jnp.clip(c - NA_KW // 2, 0, GRID_W - NA_KW)
    inside = (cp >= start) & (cp < start + NA_KW)
    diff = cp - c + (NA_KW - 1)
    n_dr = 2 * NA_KH - 1
    n_dc = 2 * NA_KW - 1
    for hh in range(HEADS_PER_BLOCK):
        h = (layer * N_HEAD_BLOCKS + hb) * HEADS_PER_BLOCK + hh
        for i in range(n_dr):
            val = jnp.zeros((GRID_W, GRID_W), F32)
            for d in range(n_dc):
                val = jnp.where(diff == d, rpb_ref[(h * n_dr + i) * n_dc + d], val)
            val = jnp.where(inside, val, -jnp.inf)
            for v in range(NA_KH):
                j = i - v
                if 0 <= j < NA_KH:
                    o_ref[0, v, 0, hh * GRID_W:(hh + 1) * GRID_W, j * GRID_W:(j + 1) * GRID_W] = val


def _window_bias(rpb):
    depth = rpb.shape[0]
    tile = (HEADS_PER_BLOCK * GRID_W, NA_KH * GRID_W)
    return pl.pallas_call(
        _bias_body,
        grid=(depth, N_HEAD_BLOCKS),
        in_specs=[pl.BlockSpec(memory_space=pltpu.SMEM)],
        out_specs=pl.BlockSpec((1, NA_KH, 1) + tile, lambda l, hb: (l, 0, hb, 0, 0)),
        out_shape=jax.ShapeDtypeStruct((depth, NA_KH, N_HEAD_BLOCKS) + tile, F32),
        compiler_params=_params(("arbitrary", "arbitrary")),
        name="window_bias",
    )(rpb.reshape(-1))


def _rope(t, cos, sin, swap_up):
    partner = jnp.where(swap_up, pltpu.roll(t, LANES - HEAD_DIM // 4, 1), pltpu.roll(t, HEAD_DIM // 4, 1))
    return t * cos + partner * sin


def _inproj_body(x_ref, mod_ref, g_ref, cos_ref, sin_ref, win_ref, u_ref, q_ref, k_ref, v_ref):
    mod = mod_ref[0, 0]
    h = _rms_modulate(x_ref[0], g_ref[0], mod[0:1], mod[1:2]).astype(BF16)
    pg = jnp.dot(h, win_ref[0, :, :GLU_END], preferred_element_type=F32)
    u_ref[0] = (pg[:, :CONV_DIM] * jax.nn.sigmoid(pg[:, CONV_DIM:])).astype(BF16)
    pq = jnp.dot(h, win_ref[0, :, GLU_END:QKV_END], preferred_element_type=F32)
    cos = cos_ref[...]
    sin = sin_ref[...]
    lane = lax.broadcasted_iota(jnp.int32, cos.shape, 1)
    swap_up = (lane % (HEAD_DIM // 2)) < (HEAD_DIM // 4)
    for blk in range(N_HEAD_BLOCKS):
        sl = slice(blk * LANES, (blk + 1) * LANES)
        q_ref[0, :, sl] = (_rope(pq[:, sl], cos, sin, swap_up) * (HEAD_DIM ** -0.5)).astype(BF16)
        ks = slice(NA_DIM + blk * LANES, NA_DIM + (blk + 1) * LANES)
        k_ref[0, :, sl] = _rope(pq[:, ks], cos, sin, swap_up).astype(BF16)
    v_ref[0] = pq[:, 2 * NA_DIM:].astype(BF16)


def _inproj(x, layer, mods, mod_row, gain, cos, sin, w_in, tm):
    b, t, d = x.shape
    seq_spec = lambda width: pl.BlockSpec((1, tm, width), lambda bi, i: (bi, i, 0))
    out = jax.ShapeDtypeStruct((b, t, NA_DIM), BF16)
    return pl.pallas_call(
        _inproj_body,
        grid=(b, t // tm),
        in_specs=[
            seq_spec(d),
            _mod_spec(layer, mod_row),
            _layer_spec(gain, layer),
            pl.BlockSpec((tm, LANES), lambda bi, i: (i, 0)),
            pl.BlockSpec((tm, LANES), lambda bi, i: (i, 0)),
            _layer_spec(w_in, layer),
        ],
        out_specs=[seq_spec(CONV_DIM), seq_spec(NA_DIM), seq_spec(NA_DIM), seq_spec(NA_DIM)],
        out_shape=[jax.ShapeDtypeStruct((b, t, CONV_DIM), BF16), out, out, out],
        compiler_params=_params(("arbitrary", "arbitrary")),
        name="inproj",
    )(x, mods, gain, cos, sin, w_in)


def _stack_heads(t):
    lane = lax.broadcasted_iota(jnp.int32, t.shape, 1)
    zero = jnp.zeros_like(t)
    return jnp.concatenate([jnp.where(lane < HEAD_DIM, t, zero), jnp.where(lane >= HEAD_DIM, t, zero)], axis=0)


def _unstack_heads(o, rows):
    lane = lax.broadcasted_iota(jnp.int32, (rows, LANES), 1)
    return jnp.where(lane < HEAD_DIM, o[:rows], o[rows:])


def _scores(lhs, keys):
    return lax.dot_general(lhs, keys, (((1,), (1,)), ((), ())), preferred_element_type=F32)


def _na_body(q_ref, k_ref, v_ref, kc_ref, vc_ref, bias_ref, o_ref):
    rows = q_ref.shape[1] // GRID_W
    band = NA_KH * GRID_W
    kc = kc_ref[0]
    vc = vc_ref[0]

    def group(g, carry):
        rs = [g * NA_ROW_GROUP + j for j in range(NA_ROW_GROUP)]
        starts = [jnp.clip(r - NA_KH // 2, 0, rows - NA_KH) for r in rs]
        q0 = [pl.multiple_of(r * GRID_W, GRID_W) for r in rs]
        k0 = [pl.multiple_of(s * GRID_W, GRID_W) for s in starts]
        lhs = [_stack_heads(q_ref[0, pl.ds(a, GRID_W), :]) for a in q0]
        variant = [s - r + (NA_KH - 1) for s, r in zip(starts, rs)]
        s_win = [_scores(l, k_ref[0, pl.ds(a, band), :]) for l, a in zip(lhs, k0)]
        s_ctx = [_scores(l, kc) for l in lhs]
        n_slabs = 2 * GRID_W // NA_SLAB
        for j in range(NA_ROW_GROUP):
            pw, pc, inv = [], [], []
            for t in range(n_slabs):
                sl = slice(t * NA_SLAB, (t + 1) * NA_SLAB)
                sw = s_win[j][sl] + bias_ref[0, variant[j], 0, sl, :]
                sc = s_ctx[j][sl]
                m = jnp.maximum(jnp.max(sw, axis=-1, keepdims=True), jnp.max(sc, axis=-1, keepdims=True))
                ew = jnp.exp(sw - m)
                ec = jnp.exp(sc - m)
                inv.append(1.0 / (jnp.sum(ew, axis=-1, keepdims=True) + jnp.sum(ec, axis=-1, keepdims=True)))
                pw.append(ew.astype(BF16))
                pc.append(ec.astype(BF16))
            o = jnp.dot(jnp.concatenate(pw, axis=0), v_ref[0, pl.ds(k0[j], band), :], preferred_element_type=F32)
            o = o + jnp.dot(jnp.concatenate(pc, axis=0), vc, preferred_element_type=F32)
            o = o * jnp.concatenate(inv, axis=0)
            o_ref[0, pl.ds(q0[j], GRID_W), :] = _unstack_heads(o, GRID_W).astype(BF16)
        return carry

    lax.fori_loop(0, rows // NA_ROW_GROUP, group, 0)


def _neighbourhood_attention(q, k, v, kc, vc, bias, layer):
    b, t, _ = q.shape
    ctx = kc.shape[1]
    assert (t // GRID_W) % NA_ROW_GROUP == 0
    seq = lambda length: pl.BlockSpec((1, length, LANES), lambda bi, hb: (bi, 0, hb))
    return pl.pallas_call(
        _na_body,
        grid=(b, N_HEAD_BLOCKS),
        in_specs=[seq(t), seq(t), seq(t), seq(ctx), seq(ctx),
                  pl.BlockSpec((1, NA_KH, 1) + bias.shape[3:], lambda bi, hb: (layer, 0, hb, 0, 0))],
        out_specs=seq(t),
        out_shape=jax.ShapeDtypeStruct((b, t, NA_DIM), BF16),
        compiler_params=_params(("arbitrary", "arbitrary")),
        name="na_attention",
    )(q, k, v, kc, vc, bias)


def _ctx_attn_body(q_ref, k_ref, v_ref, o_ref):
    n = q_ref.shape[1]
    lhs = _stack_heads(q_ref[0])
    s = _scores(lhs, k_ref[0])
    p = jnp.exp(s - jnp.max(s, axis=-1, keepdims=True))
    denom = jnp.sum(p, axis=-1, keepdims=True)
    o = jnp.dot(p.astype(BF16), v_ref[0], preferred_element_type=F32)
    o_ref[0] = _unstack_heads(o / denom, n).astype(BF16)


def _context_attention(q, k, v):
    b, n, _ = q.shape
    spec = pl.BlockSpec((1, n, LANES), lambda bi, hb: (bi, 0, hb))
    return pl.pallas_call(
        _ctx_attn_body,
        grid=(b, N_HEAD_BLOCKS),
        in_specs=[spec, spec, spec],
        out_specs=spec,
        out_shape=jax.ShapeDtypeStruct((b, n, NA_DIM), BF16),
        compiler_params=_params(("arbitrary", "arbitrary")),
        name="ctx_attention",
    )(q, k, v)


def _mix_body(x_ref, up_ref, uc_ref, un_ref, a_ref, mod_ref, g_ref, dw_ref, dwb_ref, lng_ref, lnb_ref,
              win_ref, wconv_ref, wna_ref, wout_ref, o_ref):
    i = pl.program_id(1)
    tm = x_ref.shape[1]
    prev = up_ref[0].astype(F32)
    nxt = un_ref[0].astype(F32)
    prev = jnp.where(i > 0, prev, jnp.zeros_like(prev))
    nxt = jnp.where(i < pl.num_programs(1) - 1, nxt, jnp.zeros_like(nxt))
    ucat = jnp.concatenate([prev, uc_ref[0].astype(F32), nxt], axis=0)
    n = tm + 2 * CONV_HALO
    assert CONV_HALO - CONV_K // 2 == 1 and CONV_K < 4 * SUBLANES
    acc = jnp.zeros((tm, CONV_DIM), F32) + dwb_ref[0]
    for b in range(SUBLANES):
        shifted = ucat if b == 0 else pltpu.roll(ucat, n - b, 0)
        for a in range(4):
            j = SUBLANES * a + b
            if 1 <= j <= CONV_K:
                acc = acc + dw_ref[0, j - 1:j, :] * shifted[SUBLANES * a:SUBLANES * a + tm]
    mu = jnp.mean(acc, axis=-1, keepdims=True)
    cen = acc - mu
    var = jnp.mean(cen * cen, axis=-1, keepdims=True)
    feat = jax.nn.silu(cen * lax.rsqrt(var + EPS) * lng_ref[0] + lnb_ref[0]).astype(BF16)
    y_conv = jnp.dot(feat, wconv_ref[0], preferred_element_type=F32)
    y_attn = jnp.dot(a_ref[0], wna_ref[0], preferred_element_type=F32)
    x = x_ref[0]
    mod = mod_ref[0, 0]
    h = _rms_modulate(x, g_ref[0], mod[0:1], mod[1:2]).astype(BF16)
    gates = jax.nn.sigmoid(jnp.dot(h, win_ref[0, :, QKV_END:], preferred_element_type=F32))
    merged = (gates[:, :D_MODEL] * y_conv + gates[:, D_MODEL:] * y_attn).astype(BF16)
    o_ref[0] = x + mod[2:3] * jnp.dot(merged, wout_ref[0], preferred_element_type=F32)


def _mix(x, u, attn, layer, mods, mod_row, p, tm):
    b, t, d = x.shape
    per = tm // CONV_HALO
    last = t // CONV_HALO - 1
    seq = lambda width: pl.BlockSpec((1, tm, width), lambda bi, i: (bi, i, 0))
    weights = (p["norm1_g"], p["conv_dw"], p["conv_dw_b"], p["conv_ln_g"], p["conv_ln_b"],
               p["w_in"], p["w_conv_out"], p["w_na_out"], p["w_out"])
    return pl.pallas_call(
        _mix_body,
        grid=(b, t // tm),
        in_specs=[
            seq(d),
            pl.BlockSpec((1, CONV_HALO, CONV_DIM), lambda bi, i: (bi, jnp.maximum(i * per - 1, 0), 0)),
            seq(CONV_DIM),
            pl.BlockSpec((1, CONV_HALO, CONV_DIM), lambda bi, i: (bi, jnp.minimum((i + 1) * per, last), 0)),
            seq(NA_DIM),
            _mod_spec(layer, mod_row),
        ] + [_layer_spec(w, layer) for w in weights],
        out_specs=seq(d),
        out_shape=jax.ShapeDtypeStruct((b, t, d), F32),
        compiler_params=_params(("arbitrary", "arbitrary")),
        name="mix",
    )(x, u, u, u, attn, mods, *weights)


def _ffn_body(final_norm, x_ref, xp_ref, xn_ref, mod_ref, g_ref, wup_ref, dw_ref, dwb_ref, wdown_ref, fg_ref,
              o_ref):
    i = pl.program_id(1)
    tm = x_ref.shape[1]
    mod = mod_ref[0, 0]
    gain = g_ref[0]
    x = x_ref[0]
    hp = _rms_modulate(xp_ref[0], gain, mod[3:4], mod[4:5])
    hn = _rms_modulate(xn_ref[0], gain, mod[3:4], mod[4:5])
    hp = jnp.where(i > 0, hp, jnp.zeros_like(hp))
    hn = jnp.where(i < pl.num_programs(1) - 1, hn, jnp.zeros_like(hn))
    h = jnp.concatenate([hp, _rms_modulate(x, gain, mod[3:4], mod[4:5]), hn], axis=0).astype(BF16)
    n = tm + 2 * FFN_HALO
    assert FFN_CONV_K == 3

    def conv(cols, up):
        before = pltpu.roll(up, 1, 0)[FFN_HALO:FFN_HALO + tm]
        after = pltpu.roll(up, n - 1, 0)[FFN_HALO:FFN_HALO + tm]
        return (dwb_ref[0, :, cols] + dw_ref[0, 0:1, cols] * before
                + dw_ref[0, 1:2, cols] * up[FFN_HALO:FFN_HALO + tm] + dw_ref[0, 2:3, cols] * after)

    value_cols = slice(0, FFN_DIM)
    gate_cols = slice(FFN_DIM, 2 * FFN_DIM)
    up_gate = jnp.dot(h, wup_ref[0, :, gate_cols], preferred_element_type=F32)
    up_value = jnp.dot(h, wup_ref[0, :, value_cols], preferred_element_type=F32)
    act = (jax.nn.silu(conv(gate_cols, up_gate)) * conv(value_cols, up_value)).astype(BF16)
    y = x + mod[5:6] * jnp.dot(act, wdown_ref[0], preferred_element_type=F32)
    if final_norm:
        y = y * lax.rsqrt(jnp.mean(y * y, axis=-1, keepdims=True) + EPS) * fg_ref[...]
    o_ref[0] = y


def _ffn(x, layer, mods, mod_row, p, final_gain, final_norm, tm):
    b, t, d = x.shape
    per = tm // FFN_HALO
    last = t // FFN_HALO - 1
    seq = pl.BlockSpec((1, tm, d), lambda bi, i: (bi, i, 0))
    weights = (p["norm2_g"], p["w_up"], p["ffn_dw"], p["ffn_dw_b"], p["w_down"])
    return pl.pallas_call(
        functools.partial(_ffn_body, final_norm),
        grid=(b, t // tm),
        in_specs=[
            seq,
            pl.BlockSpec((1, FFN_HALO, d), lambda bi, i: (bi, jnp.maximum(i * per - 1, 0), 0)),
            pl.BlockSpec((1, FFN_HALO, d), lambda bi, i: (bi, jnp.minimum((i + 1) * per, last), 0)),
            _mod_spec(layer, mod_row),
        ] + [_layer_spec(w, layer) for w in weights] + [
            pl.BlockSpec((1, d), lambda bi, i: (0, 0), pipeline_mode=pl.Buffered(1)),
        ],
        out_specs=seq,
        out_shape=jax.ShapeDtypeStruct((b, t, d), F32),
        compiler_params=_params(("arbitrary", "arbitrary")),
        name="ffn",
    )(x, x, x, mods, *weights, final_gain)


def _rope_tables(seq_len):
    half = HEAD_DIM // 2
    inv_freq = ROPE_THETA ** (-jnp.arange(0, half, 2, dtype=F32) / half)
    t = jnp.arange(seq_len)
    ang_r = (t // GRID_W).astype(F32)[:, None] * inv_freq[None, :]
    ang_c = (t % GRID_W).astype(F32)[:, None] * inv_freq[None, :]
    cos = jnp.concatenate([jnp.cos(ang_r)] * 2 + [jnp.cos(ang_c)] * 2, axis=-1)
    sin = jnp.concatenate([-jnp.sin(ang_r), jnp.sin(ang_r), -jnp.sin(ang_c), jnp.sin(ang_c)], axis=-1)
    return jnp.tile(cos, (1, HEADS_PER_BLOCK)), jnp.tile(sin, (1, HEADS_PER_BLOCK))


def kernel(x, c, ctx, c_ctx, w_ada, b_ada, norm1_g, w_in, conv_dw, conv_dw_b, conv_ln_g, conv_ln_b, w_conv_out,
           na_rpb, w_na_out, w_out, norm2_g, w_up, ffn_dw, ffn_dw_b, w_down, final_norm_g):
    batch, seq_len, d = x.shape
    depth = w_ada.shape[0]
    ctx_len = ctx.shape[1]
    assert d == D_MODEL and seq_len % GRID_W == 0 and batch + 1 <= MOD_ROWS

    cvec = jnp.zeros((MOD_ROWS, d), F32).at[:batch].set(c).at[batch].set(c_ctx)
    mods = _ada(cvec, w_ada, b_ada).reshape(depth, MOD_ROWS, N_MOD, d)
    lat_row = lambda bi: bi
    ctx_row = lambda bi: batch

    cos, sin = _rope_tables(seq_len)
    cos_id = jnp.ones((ctx_len, LANES), F32)
    sin_id = jnp.zeros((ctx_len, LANES), F32)
    rows = lambda v: v.reshape(depth, 1, -1)
    p = {
        "norm1_g": rows(norm1_g), "norm2_g": rows(norm2_g),
        "conv_dw": conv_dw, "conv_dw_b": rows(conv_dw_b), "conv_ln_g": rows(conv_ln_g), "conv_ln_b": rows(conv_ln_b),
        "ffn_dw": ffn_dw, "ffn_dw_b": rows(ffn_dw_b),
        "w_in": w_in.astype(BF16), "w_conv_out": w_conv_out.astype(BF16), "w_na_out": w_na_out.astype(BF16),
        "w_out": w_out.astype(BF16), "w_up": w_up.astype(BF16), "w_down": w_down.astype(BF16),
    }
    fin = final_norm_g.reshape(1, d)
    bias = _window_bias(na_rpb)

    xc = ctx
    for l in range(depth):
        last = l == depth - 1
        u_c, q_c, k_c, v_c = _inproj(xc, l, mods, ctx_row, p["norm1_g"], cos_id, sin_id, p["w_in"], ctx_len)
        u, q, k, v = _inproj(x, l, mods, lat_row, p["norm1_g"], cos, sin, p["w_in"], INPROJ_TM)
        attn = _neighbourhood_attention(q, k, v, k_c, v_c, bias, l)
        x = _mix(x, u, attn, l, mods, lat_row, p, MIX_TM)
        x = _ffn(x, l, mods, lat_row, p, fin, last, FFN_TM)
        if not last:
            attn_c = _context_attention(q_c, k_c, v_c)
            xc = _mix(xc, u_c, attn_c, l, mods, ctx_row, p, ctx_len)
            xc = _ffn(xc, l, mods, ctx_row, p, fin, False, ctx_len)
    return x
```

```python
import functools

import jax
import jax.numpy as jnp
from jax import lax
from jax.experimental import pallas as pl
from jax.experimental.pallas import tpu as pltpu

D_MODEL = 1024
GRID_W = 64
CONV_DIM = 512
CONV_K = 31
NA_HEADS = 8
HEAD_DIM = 64
NA_DIM = NA_HEADS * HEAD_DIM
NA_KH = 8
NA_KW = 16
ROPE_THETA = 10000.0
FFN_DIM = 2816
FFN_CONV_K = 3
EPS = 1e-6
GLU_END = 2 * CONV_DIM
QKV_END = GLU_END + 3 * NA_DIM

LANES = 128
SUBLANES = 8
BF16_ROWS = 16
V7X_VMEM_BYTES = 64 * 1024 * 1024
VMEM_LIMIT = V7X_VMEM_BYTES * 7 // 8

HEADS_PER_BLOCK = LANES // HEAD_DIM
N_HEAD_BLOCKS = NA_HEADS // HEADS_PER_BLOCK
CONV_HALO = BF16_ROWS
FFN_HALO = SUBLANES
MOD_ROWS = 8
N_MOD = 6
NA_ROW_GROUP = 8
NA_SLAB = 32

INPROJ_TM = 1024
MIX_TM = 512
FFN_TM = 512
ADA_TK = 256

BF16 = jnp.bfloat16
F32 = jnp.float32


def _params(semantics):
    return pltpu.CompilerParams(dimension_semantics=semantics, vmem_limit_bytes=VMEM_LIMIT)


def _layer_spec(arr, layer):
    tail = (0,) * (arr.ndim - 1)
    return pl.BlockSpec((1,) + arr.shape[1:], lambda *_: (layer,) + tail, pipeline_mode=pl.Buffered(1))


def _mod_spec(layer, mod_row):
    return pl.BlockSpec((1, 1, N_MOD, D_MODEL), lambda bi, i: (layer, mod_row(bi), 0, 0))


def _rms_modulate(x, gain, shift, scale):
    y = x * lax.rsqrt(jnp.mean(x * x, axis=-1, keepdims=True) + EPS) * gain
    return y * (1.0 + scale) + shift


def _ada_body(c_ref, w_ref, b_ref, o_ref):
    s = jax.nn.silu(c_ref[...]).astype(BF16)
    part = jnp.dot(s, w_ref[0].astype(BF16), preferred_element_type=F32)

    @pl.when(pl.program_id(1) == 0)
    def _():
        o_ref[0] = part + b_ref[0]

    @pl.when(pl.program_id(1) > 0)
    def _():
        o_ref[0] += part


def _ada(cvec, w_ada, b_ada):
    depth, d, n = w_ada.shape
    return pl.pallas_call(
        _ada_body,
        grid=(depth, d // ADA_TK),
        in_specs=[
            pl.BlockSpec((MOD_ROWS, ADA_TK), lambda l, k: (0, k)),
            pl.BlockSpec((1, ADA_TK, n), lambda l, k: (l, k, 0)),
            pl.BlockSpec((1, 1, n), lambda l, k: (l, 0, 0)),
        ],
        out_specs=pl.BlockSpec((1, MOD_ROWS, n), lambda l, k: (l, 0, 0)),
        out_shape=jax.ShapeDtypeStruct((depth, MOD_ROWS, n), F32),
        compiler_params=_params(("arbitrary", "arbitrary")),
        name="ada",
    )(cvec, w_ada, b_ada.reshape(depth, 1, n))


def _bias_body(rpb_ref, o_ref):
    layer = pl.program_id(0)
    hb = pl.program_id(1)
    c = lax.broadcasted_iota(jnp.int32, (GRID_W, GRID_W), 0)
    cp = lax.broadcasted_iota(jnp.int32, (GRID_W, GRID_W), 1)
    start = jnp.clip(c - NA_KW // 2, 0, GRID_W - NA_KW)
    inside = (cp >= start) & (cp < start + NA_KW)
    diff = cp - c + (NA_KW - 1)
    n_dr = 2 * NA_KH - 1
    n_dc = 2 * NA_KW - 1
    for hh in range(HEADS_PER_BLOCK):
        h = (layer * N_HEAD_BLOCKS + hb) * HEADS_PER_BLOCK + hh
        for i in range(n_dr):
            val = jnp.zeros((GRID_W, GRID_W), F32)
            for d in range(n_dc):
                val = jnp.where(diff == d, rpb_ref[(h * n_dr + i) * n_dc + d], val)
            val = jnp.where(inside, val, -jnp.inf)
            for v in range(NA_KH):
                j = i - v
                if 0 <= j < NA_KH:
                    o_ref[0, v, 0, hh * GRID_W:(hh + 1) * GRID_W, j * GRID_W:(j + 1) * GRID_W] = val


def _window_bias(rpb):
    depth = rpb.shape[0]
    tile = (HEADS_PER_BLOCK * GRID_W, NA_KH * GRID_W)
    return pl.pallas_call(
        _bias_body,
        grid=(depth, N_HEAD_BLOCKS),
        in_specs=[pl.BlockSpec(memory_space=pltpu.SMEM)],
        out_specs=pl.BlockSpec((1, NA_KH, 1) + tile, lambda l, hb: (l, 0, hb, 0, 0)),
        out_shape=jax.ShapeDtypeStruct((depth, NA_KH, N_HEAD_BLOCKS) + tile, F32),
        compiler_params=_params(("arbitrary", "arbitrary")),
        name="window_bias",
    )(rpb.reshape(-1))


def _rope(t, cos, sin, swap_up):
    partner = jnp.where(swap_up, pltpu.roll(t, LANES - HEAD_DIM // 4, 1), pltpu.roll(t, HEAD_DIM // 4, 1))
    return t * cos + partner * sin


def _inproj_body(x_ref, mod_ref, g_ref, cos_ref, sin_ref, win_ref, u_ref, q_ref, k_ref, v_ref):
    mod = mod_ref[0, 0]
    h = _rms_modulate(x_ref[0], g_ref[0], mod[0:1], mod[1:2]).astype(BF16)
    pg = jnp.dot(h, win_ref[0, :, :GLU_END], preferred_element_type=F32)
    u_ref[0] = (pg[:, :CONV_DIM] * jax.nn.sigmoid(pg[:, CONV_DIM:])).astype(BF16)
    pq = jnp.dot(h, win_ref[0, :, GLU_END:QKV_END], preferred_element_type=F32)
    cos = cos_ref[...]
    sin = sin_ref[...]
    lane = lax.broadcasted_iota(jnp.int32, cos.shape, 1)
    swap_up = (lane % (HEAD_DIM // 2)) < (HEAD_DIM // 4)
    for blk in range(N_HEAD_BLOCKS):
        sl = slice(blk * LANES, (blk + 1) * LANES)
        q_ref[0, :, sl] = (_rope(pq[:, sl], cos, sin, swap_up) * (HEAD_DIM ** -0.5)).astype(BF16)
        ks = slice(NA_DIM + blk * LANES, NA_DIM + (blk + 1) * LANES)
        k_ref[0, :, sl] = _rope(pq[:, ks], cos, sin, swap_up).astype(BF16)
    v_ref[0] = pq[:, 2 * NA_DIM:].astype(BF16)


def _inproj(x, layer, mods, mod_row, gain, cos, sin, w_in, tm):
    b, t, d = x.shape
    seq_spec = lambda width: pl.BlockSpec((1, tm, width), lambda bi, i: (bi, i, 0))
    out = jax.ShapeDtypeStruct((b, t, NA_DIM), BF16)
    return pl.pallas_call(
        _inproj_body,
        grid=(b, t // tm),
        in_specs=[
            seq_spec(d),
            _mod_spec(layer, mod_row),
            _layer_spec(gain, layer),
            pl.BlockSpec((tm, LANES), lambda bi, i: (i, 0)),
            pl.BlockSpec((tm, LANES), lambda bi, i: (i, 0)),
            _layer_spec(w_in, layer),
        ],
        out_specs=[seq_spec(CONV_DIM), seq_spec(NA_DIM), seq_spec(NA_DIM), seq_spec(NA_DIM)],
        out_shape=[jax.ShapeDtypeStruct((b, t, CONV_DIM), BF16), out, out, out],
        compiler_params=_params(("arbitrary", "arbitrary")),
        name="inproj",
    )(x, mods, gain, cos, sin, w_in)


def _stack_heads(t):
    lane = lax.broadcasted_iota(jnp.int32, t.shape, 1)
    zero = jnp.zeros_like(t)
    return jnp.concatenate([jnp.where(lane < HEAD_DIM, t, zero), jnp.where(lane >= HEAD_DIM, t, zero)], axis=0)


def _unstack_heads(o, rows):
    lane = lax.broadcasted_iota(jnp.int32, (rows, LANES), 1)
    return jnp.where(lane < HEAD_DIM, o[:rows], o[rows:])


def _scores(lhs, keys):
    return lax.dot_general(lhs, keys, (((1,), (1,)), ((), ())), preferred_element_type=F32)


def _na_body(q_ref, k_ref, v_ref, kc_ref, vc_ref, bias_ref, o_ref):
    rows = q_ref.shape[1] // GRID_W
    band = NA_KH * GRID_W
    kc = kc_ref[0]
    vc = vc_ref[0]

    def group(g, carry):
        rs = [g * NA_ROW_GROUP + j for j in range(NA_ROW_GROUP)]
        starts = [jnp.clip(r - NA_KH // 2, 0, rows - NA_KH) for r in rs]
        q0 = [pl.multiple_of(r * GRID_W, GRID_W) for r in rs]
        k0 = [pl.multiple_of(s * GRID_W, GRID_W) for s in starts]
        lhs = [_stack_heads(q_ref[0, pl.ds(a, GRID_W), :]) for a in q0]
        variant = [s - r + (NA_KH - 1) for s, r in zip(starts, rs)]
        s_win = [_scores(l, k_ref[0, pl.ds(a, band), :]) for l, a in zip(lhs, k0)]
        s_ctx = [_scores(l, kc) for l in lhs]
        n_slabs = 2 * GRID_W // NA_SLAB
        for j in range(NA_ROW_GROUP):
            pw, pc, inv = [], [], []
            for t in range(n_slabs):
                sl = slice(t * NA_SLAB, (t + 1) * NA_SLAB)
                sw = s_win[j][sl] + bias_ref[0, variant[j], 0, sl, :]
                sc = s_ctx[j][sl]
                m = jnp.maximum(jnp.max(sw, axis=-1, keepdims=True), jnp.max(sc, axis=-1, keepdims=True))
                ew = jnp.exp(sw - m)
                ec = jnp.exp(sc - m)
                inv.append(1.0 / (jnp.sum(ew, axis=-1, keepdims=True) + jnp.sum(ec, axis=-1, keepdims=True)))
                pw.append(ew.astype(BF16))
                pc.append(ec.astype(BF16))
            o = jnp.dot(jnp.concatenate(pw, axis=0), v_ref[0, pl.ds(k0[j], band), :], preferred_element_type=F32)
            o = o + jnp.dot(jnp.concatenate(pc, axis=0), vc, preferred_element_type=F32)
            o = o * jnp.concatenate(inv, axis=0)
            o_ref[0, pl.ds(q0[j], GRID_W), :] = _unstack_heads(o, GRID_W).astype(BF16)
        return carry

    lax.fori_loop(0, rows // NA_ROW_GROUP, group, 0)


def _neighbourhood_attention(q, k, v, kc, vc, bias, layer):
    b, t, _ = q.shape
    ctx = kc.shape[1]
    assert (t // GRID_W) % NA_ROW_GROUP == 0
    seq = lambda length: pl.BlockSpec((1, length, LANES), lambda bi, hb: (bi, 0, hb))
    return pl.pallas_call(
        _na_body,
        grid=(b, N_HEAD_BLOCKS),
        in_specs=[seq(t), seq(t), seq(t), seq(ctx), seq(ctx),
                  pl.BlockSpec((1, NA_KH, 1) + bias.shape[3:], lambda bi, hb: (layer, 0, hb, 0, 0))],
        out_specs=seq(t),
        out_shape=jax.ShapeDtypeStruct((b, t, NA_DIM), BF16),
        compiler_params=_params(("arbitrary", "arbitrary")),
        name="na_attention",
    )(q, k, v, kc, vc, bias)


def _ctx_attn_body(q_ref, k_ref, v_ref, o_ref):
    n = q_ref.shape[1]
    lhs = _stack_heads(q_ref[0])
    s = _scores(lhs, k_ref[0])
    p = jnp.exp(s - jnp.max(s, axis=-1, keepdims=True))
    denom = jnp.sum(p, axis=-1, keepdims=True)
    o = jnp.dot(p.astype(BF16), v_ref[0], preferred_element_type=F32)
    o_ref[0] = _unstack_heads(o / denom, n).astype(BF16)


def _context_attention(q, k, v):
    b, n, _ = q.shape
    spec = pl.BlockSpec((1, n, LANES), lambda bi, hb: (bi, 0, hb))
    return pl.pallas_call(
        _ctx_attn_body,
        grid=(b, N_HEAD_BLOCKS),
        in_specs=[spec, spec, spec],
        out_specs=spec,
        out_shape=jax.ShapeDtypeStruct((b, n, NA_DIM), BF16),
        compiler_params=_params(("arbitrary", "arbitrary")),
        name="ctx_attention",
    )(q, k, v)


def _mix_body(x_ref, up_ref, uc_ref, un_ref, a_ref, mod_ref, g_ref, dw_ref, dwb_ref, lng_ref, lnb_ref,
              win_ref, wconv_ref, wna_ref, wout_ref, o_ref):
    i = pl.program_id(1)
    tm = x_ref.shape[1]
    x = x_ref[0]
    mod = mod_ref[0, 0]
    h = _rms_modulate(x, g_ref[0], mod[0:1], mod[1:2]).astype(BF16)
    gates = jax.nn.sigmoid(jnp.dot(h, win_ref[0, :, QKV_END:], preferred_element_type=F32))
    y_attn = jnp.dot(a_ref[0], wna_ref[0], preferred_element_type=F32)
    prev = up_ref[0].astype(F32)
    nxt = un_ref[0].astype(F32)
    prev = jnp.where(i > 0, prev, jnp.zeros_like(prev))
    nxt = jnp.where(i < pl.num_programs(1) - 1, nxt, jnp.zeros_like(nxt))
    ucat = jnp.concatenate([prev, uc_ref[0].astype(F32), nxt], axis=0)
    n = tm + 2 * CONV_HALO
    assert CONV_HALO - CONV_K // 2 == 1 and CONV_K < 4 * SUBLANES
    acc = jnp.zeros((tm, CONV_DIM), F32) + dwb_ref[0]
    for b in range(SUBLANES):
        shifted = ucat if b == 0 else pltpu.roll(ucat, n - b, 0)
        for a in range(4):
            j = SUBLANES * a + b
            if 1 <= j <= CONV_K:
                acc = acc + dw_ref[0, j - 1:j, :] * shifted[SUBLANES * a:SUBLANES * a + tm]
    mu = jnp.mean(acc, axis=-1, keepdims=True)
    cen = acc - mu
    var = jnp.mean(cen * cen, axis=-1, keepdims=True)
    feat = jax.nn.silu(cen * lax.rsqrt(var + EPS) * lng_ref[0] + lnb_ref[0]).astype(BF16)
    y_conv = jnp.dot(feat, wconv_ref[0], preferred_element_type=F32)
    merged = (gates[:, :D_MODEL] * y_conv + gates[:, D_MODEL:] * y_attn).astype(BF16)
    o_ref[0] = x + mod[2:3] * jnp.dot(merged, wout_ref[0], preferred_element_type=F32)


def _mix(x, u, attn, layer, mods, mod_row, p, tm):
    b, t, d = x.shape
    per = tm // CONV_HALO
    last = t // CONV_HALO - 1
    seq = lambda width: pl.BlockSpec((1, tm, width), lambda bi, i: (bi, i, 0))
    weights = (p["norm1_g"], p["conv_dw"], p["conv_dw_b"], p["conv_ln_g"], p["conv_ln_b"],
               p["w_in"], p["w_conv_out"], p["w_na_out"], p["w_out"])
    return pl.pallas_call(
        _mix_body,
        grid=(b, t // tm),
        in_specs=[
            seq(d),
            pl.BlockSpec((1, CONV_HALO, CONV_DIM), lambda bi, i: (bi, jnp.maximum(i * per - 1, 0), 0)),
            seq(CONV_DIM),
            pl.BlockSpec((1, CONV_HALO, CONV_DIM), lambda bi, i: (bi, jnp.minimum((i + 1) * per, last), 0)),
            seq(NA_DIM),
            _mod_spec(layer, mod_row),
        ] + [_layer_spec(w, layer) for w in weights],
        out_specs=seq(d),
        out_shape=jax.ShapeDtypeStruct((b, t, d), F32),
        compiler_params=_params(("arbitrary", "arbitrary")),
        name="mix",
    )(x, u, u, u, attn, mods, *weights)


def _ffn_body(final_norm, x_ref, xp_ref, xn_ref, mod_ref, g_ref, wup_ref, dw_ref, dwb_ref, wdown_ref, fg_ref,
              o_ref):
    i = pl.program_id(1)
    tm = x_ref.shape[1]
    mod = mod_ref[0, 0]
    gain = g_ref[0]
    x = x_ref[0]
    hp = _rms_modulate(xp_ref[0], gain, mod[3:4], mod[4:5])
    hn = _rms_modulate(xn_ref[0], gain, mod[3:4], mod[4:5])
    hp = jnp.where(i > 0, hp, jnp.zeros_like(hp))
    hn = jnp.where(i < pl.num_programs(1) - 1, hn, jnp.zeros_like(hn))
    h = jnp.concatenate([hp, _rms_modulate(x, gain, mod[3:4], mod[4:5]), hn], axis=0).astype(BF16)
    n = tm + 2 * FFN_HALO
    assert FFN_CONV_K == 3

    def conv(cols, up):
        before = pltpu.roll(up, 1, 0)[FFN_HALO:FFN_HALO + tm]
        after = pltpu.roll(up, n - 1, 0)[FFN_HALO:FFN_HALO + tm]
        return (dwb_ref[0, :, cols] + dw_ref[0, 0:1, cols] * before
                + dw_ref[0, 1:2, cols] * up[FFN_HALO:FFN_HALO + tm] + dw_ref[0, 2:3, cols] * after)

    value_cols = slice(0, FFN_DIM)
    gate_cols = slice(FFN_DIM, 2 * FFN_DIM)
    up_gate = jnp.dot(h, wup_ref[0, :, gate_cols], preferred_element_type=F32)
    up_value = jnp.dot(h, wup_ref[0, :, value_cols], preferred_element_type=F32)
    act = (jax.nn.silu(conv(gate_cols, up_gate)) * conv(value_cols, up_value)).astype(BF16)
    y = x + mod[5:6] * jnp.dot(act, wdown_ref[0], preferred_element_type=F32)
    if final_norm:
        y = y * lax.rsqrt(jnp.mean(y * y, axis=-1, keepdims=True) + EPS) * fg_ref[...]
    o_ref[0] = y


def _ffn(x, layer, mods, mod_row, p, final_gain, final_norm, tm):
    b, t, d = x.shape
    per = tm // FFN_HALO
    last = t // FFN_HALO - 1
    seq = pl.BlockSpec((1, tm, d), lambda bi, i: (bi, i, 0))
    weights = (p["norm2_g"], p["w_up"], p["ffn_dw"], p["ffn_dw_b"], p["w_down"])
    return pl.pallas_call(
        functools.partial(_ffn_body, final_norm),
        grid=(b, t // tm),
        in_specs=[
            seq,
            pl.BlockSpec((1, FFN_HALO, d), lambda bi, i: (bi, jnp.maximum(i * per - 1, 0), 0)),
            pl.BlockSpec((1, FFN_HALO, d), lambda bi, i: (bi, jnp.minimum((i + 1) * per, last), 0)),
            _mod_spec(layer, mod_row),
        ] + [_layer_spec(w, layer) for w in weights] + [
            pl.BlockSpec((1, d), lambda bi, i: (0, 0), pipeline_mode=pl.Buffered(1)),
        ],
        out_specs=seq,
        out_shape=jax.ShapeDtypeStruct((b, t, d), F32),
        compiler_params=_params(("arbitrary", "arbitrary")),
        name="ffn",
    )(x, x, x, mods, *weights, final_gain)


def _rope_tables(seq_len):
    half = HEAD_DIM // 2
    inv_freq = ROPE_THETA ** (-jnp.arange(0, half, 2, dtype=F32) / half)
    t = jnp.arange(seq_len)
    ang_r = (t // GRID_W).astype(F32)[:, None] * inv_freq[None, :]
    ang_c = (t % GRID_W).astype(F32)[:, None] * inv_freq[None, :]
    cos = jnp.concatenate([jnp.cos(ang_r)] * 2 + [jnp.cos(ang_c)] * 2, axis=-1)
    sin = jnp.concatenate([-jnp.sin(ang_r), jnp.sin(ang_r), -jnp.sin(ang_c), jnp.sin(ang_c)], axis=-1)
    return jnp.tile(cos, (1, HEADS_PER_BLOCK)), jnp.tile(sin, (1, HEADS_PER_BLOCK))


def kernel(x, c, ctx, c_ctx, w_ada, b_ada, norm1_g, w_in, conv_dw, conv_dw_b, conv_ln_g, conv_ln_b, w_conv_out,
           na_rpb, w_na_out, w_out, norm2_g, w_up, ffn_dw, ffn_dw_b, w_down, final_norm_g):
    batch, seq_len, d = x.shape
    depth = w_ada.shape[0]
    ctx_len = ctx.shape[1]
    assert d == D_MODEL and seq_len % GRID_W == 0 and batch + 1 <= MOD_ROWS

    cvec = jnp.zeros((MOD_ROWS, d), F32).at[:batch].set(c).at[batch].set(c_ctx)
    mods = _ada(cvec, w_ada, b_ada).reshape(depth, MOD_ROWS, N_MOD, d)
    lat_row = lambda bi: bi
    ctx_row = lambda bi: batch

    cos, sin = _rope_tables(seq_len)
    cos_id = jnp.ones((ctx_len, LANES), F32)
    sin_id = jnp.zeros((ctx_len, LANES), F32)
    rows = lambda v: v.reshape(depth, 1, -1)
    p = {
        "norm1_g": rows(norm1_g), "norm2_g": rows(norm2_g),
        "conv_dw": conv_dw, "conv_dw_b": rows(conv_dw_b), "conv_ln_g": rows(conv_ln_g), "conv_ln_b": rows(conv_ln_b),
        "ffn_dw": ffn_dw, "ffn_dw_b": rows(ffn_dw_b),
        "w_in": w_in.astype(BF16), "w_conv_out": w_conv_out.astype(BF16), "w_na_out": w_na_out.astype(BF16),
        "w_out": w_out.astype(BF16), "w_up": w_up.astype(BF16), "w_down": w_down.astype(BF16),
    }
    fin = final_norm_g.reshape(1, d)
    bias = _window_bias(na_rpb)

    xc = ctx
    for l in range(depth):
        last = l == depth - 1
        u_c, q_c, k_c, v_c = _inproj(xc, l, mods, ctx_row, p["norm1_g"], cos_id, sin_id, p["w_in"], ctx_len)
        u, q, k, v = _inproj(x, l, mods, lat_row, p["norm1_g"], cos, sin, p["w_in"], INPROJ_TM)
        attn = _neighbourhood_attention(q, k, v, k_c, v_c, bias, l)
        x = _mix(x, u, attn, l, mods, lat_row, p, MIX_TM)
        x = _ffn(x, l, mods, lat_row, p, fin, last, FFN_TM)
        if not last:
            attn_c = _context_attention(q_c, k_c, v_c)
            xc = _mix(xc, u_c, attn_c, l, mods, ctx_row, p, ctx_len)
            xc = _ffn(xc, l, mods, ctx_row, p, fin, False, ctx_len)
    return x
```

```python
import functools

import jax
import jax.numpy as jnp
from jax import lax
from jax.experimental import pallas as pl
from jax.experimental.pallas import tpu as pltpu

D_MODEL = 1024
GRID_W = 64
CONV_DIM = 512
CONV_K = 31
NA_HEADS = 8
HEAD_DIM = 64
NA_DIM = NA_HEADS * HEAD_DIM
NA_KH = 8
NA_KW = 16
ROPE_THETA = 10000.0
FFN_DIM = 2816
FFN_CONV_K = 3
EPS = 1e-6
GLU_END = 2 * CONV_DIM
QKV_END = GLU_END + 3 * NA_DIM

LANES = 128
SUBLANES = 8
BF16_ROWS = 16
V7X_VMEM_BYTES = 64 * 1024 * 1024
VMEM_LIMIT = V7X_VMEM_BYTES * 7 // 8

HEADS_PER_BLOCK = LANES // HEAD_DIM
N_HEAD_BLOCKS = NA_HEADS // HEADS_PER_BLOCK
CONV_HALO = BF16_ROWS
FFN_HALO = SUBLANES
MOD_ROWS = 8
N_MOD = 6
NA_ROW_GROUP = 8
NA_SLAB = 32

INPROJ_TM = 1024
MIX_TM = 512
FFN_TM = 512
ADA_TK = 256

BF16 = jnp.bfloat16
F32 = jnp.float32


def _params(semantics):
    return pltpu.CompilerParams(dimension_semantics=semantics, vmem_limit_bytes=VMEM_LIMIT)


def _layer_spec(arr, layer):
    tail = (0,) * (arr.ndim - 1)
    return pl.BlockSpec((1,) + arr.shape[1:], lambda *_: (layer,) + tail, pipeline_mode=pl.Buffered(1))


def _whole_spec(arr):
    zeros = (0,) * arr.ndim
    return pl.BlockSpec(arr.shape, lambda *_: zeros, pipeline_mode=pl.Buffered(1))


def _mod_spec(layer, mod_row):
    return pl.BlockSpec((1, 1, N_MOD, D_MODEL), lambda bi, i: (layer, mod_row(bi), 0, 0))


def _rms_modulate(x, gain, shift, scale):
    return x * lax.rsqrt(jnp.mean(x * x, axis=-1, keepdims=True) + EPS) * (gain * (1.0 + scale)) + shift


def _ada_body(c_ref, w_ref, b_ref, o_ref):
    s = jax.nn.silu(c_ref[...]).astype(BF16)
    part = jnp.dot(s, w_ref[0].astype(BF16), preferred_element_type=F32)

    @pl.when(pl.program_id(1) == 0)
    def _():
        o_ref[0] = part + b_ref[0]

    @pl.when(pl.program_id(1) > 0)
    def _():
        o_ref[0] += part


def _ada(cvec, w_ada, b_ada):
    depth, d, n = w_ada.shape
    return pl.pallas_call(
        _ada_body,
        grid=(depth, d // ADA_TK),
        in_specs=[
            pl.BlockSpec((MOD_ROWS, ADA_TK), lambda l, k: (0, k)),
            pl.BlockSpec((1, ADA_TK, n), lambda l, k: (l, k, 0)),
            pl.BlockSpec((1, 1, n), lambda l, k: (l, 0, 0)),
        ],
        out_specs=pl.BlockSpec((1, MOD_ROWS, n), lambda l, k: (l, 0, 0)),
        out_shape=jax.ShapeDtypeStruct((depth, MOD_ROWS, n), F32),
        compiler_params=_params(("arbitrary", "arbitrary")),
        name="ada",
    )(cvec, w_ada, b_ada.reshape(depth, 1, n))


def _bias_body(rpb_ref, o_ref):
    layer = pl.program_id(0)
    hb = pl.program_id(1)
    c = lax.broadcasted_iota(jnp.int32, (GRID_W, GRID_W), 0)
    cp = lax.broadcasted_iota(jnp.int32, (GRID_W, GRID_W), 1)
    start = jnp.clip(c - NA_KW // 2, 0, GRID_W - NA_KW)
    inside = (cp >= start) & (cp < start + NA_KW)
    diff = cp - c + (NA_KW - 1)
    n_dr = 2 * NA_KH - 1
    n_dc = 2 * NA_KW - 1
    for hh in range(HEADS_PER_BLOCK):
        h = (layer * N_HEAD_BLOCKS + hb) * HEADS_PER_BLOCK + hh
        for i in range(n_dr):
            val = jnp.zeros((GRID_W, GRID_W), F32)
            for d in range(n_dc):
                val = jnp.where(diff == d, rpb_ref[(h * n_dr + i) * n_dc + d], val)
            val = jnp.where(inside, val, -jnp.inf)
            for v in range(NA_KH):
                j = i - v
                if 0 <= j < NA_KH:
                    o_ref[0, v, 0, hh * GRID_W:(hh + 1) * GRID_W, j * GRID_W:(j + 1) * GRID_W] = val


def _window_bias(rpb):
    depth = rpb.shape[0]
    tile = (HEADS_PER_BLOCK * GRID_W, NA_KH * GRID_W)
    return pl.pallas_call(
        _bias_body,
        grid=(depth, N_HEAD_BLOCKS),
        in_specs=[pl.BlockSpec(memory_space=pltpu.SMEM)],
        out_specs=pl.BlockSpec((1, NA_KH, 1) + tile, lambda l, hb: (l, 0, hb, 0, 0)),
        out_shape=jax.ShapeDtypeStruct((depth, NA_KH, N_HEAD_BLOCKS) + tile, F32),
        compiler_params=_params(("arbitrary", "arbitrary")),
        name="window_bias",
    )(rpb.reshape(-1))


def _rope(t, cos, sin, swap_up):
    partner = jnp.where(swap_up, pltpu.roll(t, LANES - HEAD_DIM // 4, 1), pltpu.roll(t, HEAD_DIM // 4, 1))
    return t * cos + partner * sin


def _inproj_body(cast_weights, x_ref, mod_ref, g_ref, cos_ref, sin_ref, win_ref, u_ref, q_ref, k_ref, v_ref,
                 *wb_refs):
    if cast_weights:
        (w_ref,) = wb_refs

        @pl.when((pl.program_id(0) == 0) & (pl.program_id(1) == 0))
        def _():
            w_ref[...] = win_ref[0].astype(BF16)
    else:
        w_ref = win_ref
    mod = mod_ref[0, 0]
    h = _rms_modulate(x_ref[0], g_ref[0], mod[0:1], mod[1:2]).astype(BF16)
    pg = jnp.dot(h, w_ref[:, :GLU_END], preferred_element_type=F32)
    u_ref[0] = (pg[:, :CONV_DIM] * jax.nn.sigmoid(pg[:, CONV_DIM:])).astype(BF16)
    pq = jnp.dot(h, w_ref[:, GLU_END:QKV_END], preferred_element_type=F32)
    cos = cos_ref[...]
    sin = sin_ref[...]
    lane = lax.broadcasted_iota(jnp.int32, cos.shape, 1)
    swap_up = (lane % (HEAD_DIM // 2)) < (HEAD_DIM // 4)
    for blk in range(N_HEAD_BLOCKS):
        sl = slice(blk * LANES, (blk + 1) * LANES)
        q_ref[0, :, sl] = (_rope(pq[:, sl], cos, sin, swap_up) * (HEAD_DIM ** -0.5)).astype(BF16)
        ks = slice(NA_DIM + blk * LANES, NA_DIM + (blk + 1) * LANES)
        k_ref[0, :, sl] = _rope(pq[:, ks], cos, sin, swap_up).astype(BF16)
    v_ref[0] = pq[:, 2 * NA_DIM:].astype(BF16)


def _inproj(x, layer, mods, mod_row, gain, cos, sin, w_in, tm):
    b, t, d = x.shape
    cast_weights = w_in.ndim == 3
    seq_spec = lambda width: pl.BlockSpec((1, tm, width), lambda bi, i: (bi, i, 0))
    out = jax.ShapeDtypeStruct((b, t, NA_DIM), BF16)
    out_specs = [seq_spec(CONV_DIM), seq_spec(NA_DIM), seq_spec(NA_DIM), seq_spec(NA_DIM)]
    out_shape = [jax.ShapeDtypeStruct((b, t, CONV_DIM), BF16), out, out, out]
    if cast_weights:
        out_specs.append(pl.BlockSpec(w_in.shape[1:], lambda bi, i: (0, 0)))
        out_shape.append(jax.ShapeDtypeStruct(w_in.shape[1:], BF16))
    return pl.pallas_call(
        functools.partial(_inproj_body, cast_weights),
        grid=(b, t // tm),
        in_specs=[
            seq_spec(d),
            _mod_spec(layer, mod_row),
            _layer_spec(gain, layer),
            pl.BlockSpec((tm, LANES), lambda bi, i: (i, 0)),
            pl.BlockSpec((tm, LANES), lambda bi, i: (i, 0)),
            _layer_spec(w_in, layer) if cast_weights else _whole_spec(w_in),
        ],
        out_specs=out_specs,
        out_shape=out_shape,
        compiler_params=_params(("arbitrary", "arbitrary")),
        name="inproj",
    )(x, mods, gain, cos, sin, w_in)


def _stack_heads(t):
    lane = lax.broadcasted_iota(jnp.int32, t.shape, 1)
    zero = jnp.zeros_like(t)
    return jnp.concatenate([jnp.where(lane < HEAD_DIM, t, zero), jnp.where(lane >= HEAD_DIM, t, zero)], axis=0)


def _unstack_heads(o, rows):
    lane = lax.broadcasted_iota(jnp.int32, (rows, LANES), 1)
    return jnp.where(lane < HEAD_DIM, o[:rows], o[rows:])


def _scores(lhs, keys):
    return lax.dot_general(lhs, keys, (((1,), (1,)), ((), ())), preferred_element_type=F32)


def _na_body(q_ref, k_ref, v_ref, kc_ref, vc_ref, bias_ref, wup_ref, wdown_ref, o_ref, wup_bf_ref, wdown_bf_ref):
    wup_bf_ref[...] = wup_ref[0].astype(BF16)
    wdown_bf_ref[...] = wdown_ref[0].astype(BF16)
    rows = q_ref.shape[1] // GRID_W
    band = NA_KH * GRID_W
    kc = kc_ref[0]
    vc = vc_ref[0]

    def group(g, carry):
        rs = [g * NA_ROW_GROUP + j for j in range(NA_ROW_GROUP)]
        starts = [jnp.clip(r - NA_KH // 2, 0, rows - NA_KH) for r in rs]
        q0 = [pl.multiple_of(r * GRID_W, GRID_W) for r in rs]
        k0 = [pl.multiple_of(s * GRID_W, GRID_W) for s in starts]
        lhs = [_stack_heads(q_ref[0, pl.ds(a, GRID_W), :]) for a in q0]
        variant = [s - r + (NA_KH - 1) for s, r in zip(starts, rs)]
        s_win = [_scores(l, k_ref[0, pl.ds(a, band), :]) for l, a in zip(lhs, k0)]
        s_ctx = [_scores(l, kc) for l in lhs]
        n_slabs = 2 * GRID_W // NA_SLAB
        for j in range(NA_ROW_GROUP):
            pw, pc, inv = [], [], []
            for t in range(n_slabs):
                sl = slice(t * NA_SLAB, (t + 1) * NA_SLAB)
                sw = s_win[j][sl] + bias_ref[0, variant[j], 0, sl, :]
                sc = s_ctx[j][sl]
                m = jnp.maximum(jnp.max(sw, axis=-1, keepdims=True), jnp.max(sc, axis=-1, keepdims=True))
                ew = jnp.exp(sw - m)
                ec = jnp.exp(sc - m)
                inv.append(1.0 / (jnp.sum(ew, axis=-1, keepdims=True) + jnp.sum(ec, axis=-1, keepdims=True)))
                pw.append(ew.astype(BF16))
                pc.append(ec.astype(BF16))
            o = jnp.dot(jnp.concatenate(pw, axis=0), v_ref[0, pl.ds(k0[j], band), :], preferred_element_type=F32)
            o = o + jnp.dot(jnp.concatenate(pc, axis=0), vc, preferred_element_type=F32)
            o = o * jnp.concatenate(inv, axis=0)
            o_ref[0, pl.ds(q0[j], GRID_W), :] = _unstack_heads(o, GRID_W).astype(BF16)
        return carry

    lax.fori_loop(0, rows // NA_ROW_GROUP, group, 0)


def _neighbourhood_attention(q, k, v, kc, vc, bias, w_up, w_down, layer):
    b, t, _ = q.shape
    ctx = kc.shape[1]
    steps = b * N_HEAD_BLOCKS
    assert (t // GRID_W) % NA_ROW_GROUP == 0
    seq = lambda length: pl.BlockSpec((1, length, LANES), lambda bi, hb: (bi, 0, hb))

    def slab_specs(w):
        rows, cols = w.shape[1:]
        assert rows % (steps * BF16_ROWS) == 0
        slab = rows // steps
        return (pl.BlockSpec((1, slab, cols), lambda bi, hb: (layer, bi * N_HEAD_BLOCKS + hb, 0)),
                pl.BlockSpec((slab, cols), lambda bi, hb: (bi * N_HEAD_BLOCKS + hb, 0)),
                jax.ShapeDtypeStruct((rows, cols), BF16))

    up_in, up_out, up_shape = slab_specs(w_up)
    down_in, down_out, down_shape = slab_specs(w_down)
    return pl.pallas_call(
        _na_body,
        grid=(b, N_HEAD_BLOCKS),
        in_specs=[seq(t), seq(t), seq(t), seq(ctx), seq(ctx),
                  pl.BlockSpec((1, NA_KH, 1) + bias.shape[3:], lambda bi, hb: (layer, 0, hb, 0, 0)),
                  up_in, down_in],
        out_specs=[seq(t), up_out, down_out],
        out_shape=[jax.ShapeDtypeStruct((b, t, NA_DIM), BF16), up_shape, down_shape],
        compiler_params=_params(("arbitrary", "arbitrary")),
        name="na_attention",
    )(q, k, v, kc, vc, bias, w_up, w_down)


def _ctx_attn_body(q_ref, k_ref, v_ref, o_ref):
    n = q_ref.shape[1]
    lhs = _stack_heads(q_ref[0])
    s = _scores(lhs, k_ref[0])
    p = jnp.exp(s - jnp.max(s, axis=-1, keepdims=True))
    denom = jnp.sum(p, axis=-1, keepdims=True)
    o = jnp.dot(p.astype(BF16), v_ref[0], preferred_element_type=F32)
    o_ref[0] = _unstack_heads(o / denom, n).astype(BF16)


def _context_attention(q, k, v):
    b, n, _ = q.shape
    spec = pl.BlockSpec((1, n, LANES), lambda bi, hb: (bi, 0, hb))
    return pl.pallas_call(
        _ctx_attn_body,
        grid=(b, N_HEAD_BLOCKS),
        in_specs=[spec, spec, spec],
        out_specs=spec,
        out_shape=jax.ShapeDtypeStruct((b, n, NA_DIM), BF16),
        compiler_params=_params(("arbitrary", "arbitrary")),
        name="ctx_attention",
    )(q, k, v)


def _mix_body(x_ref, up_ref, uc_ref, un_ref, a_ref, mod_ref, win_ref, g_ref, dw_ref, dwb_ref, lng_ref, lnb_ref,
              wconv_ref, wna_ref, wout_ref, o_ref):
    i = pl.program_id(1)
    tm = x_ref.shape[1]
    x = x_ref[0]
    mod = mod_ref[0, 0]
    h = _rms_modulate(x, g_ref[0], mod[0:1], mod[1:2]).astype(BF16)
    gates = jax.nn.sigmoid(jnp.dot(h, win_ref[:, QKV_END:], preferred_element_type=F32))
    y_attn = jnp.dot(a_ref[0], wna_ref[0], preferred_element_type=F32)
    prev = up_ref[0].astype(F32)
    nxt = un_ref[0].astype(F32)
    prev = jnp.where(i > 0, prev, jnp.zeros_like(prev))
    nxt = jnp.where(i < pl.num_programs(1) - 1, nxt, jnp.zeros_like(nxt))
    ucat = jnp.concatenate([prev, uc_ref[0].astype(F32), nxt], axis=0)
    n = tm + 2 * CONV_HALO
    assert CONV_HALO - CONV_K // 2 == 1 and CONV_K < 4 * SUBLANES
    acc = jnp.zeros((tm, CONV_DIM), F32) + dwb_ref[0]
    for b in range(SUBLANES):
        shifted = ucat if b == 0 else pltpu.roll(ucat, n - b, 0)
        for a in range(4):
            j = SUBLANES * a + b
            if 1 <= j <= CONV_K:
                acc = acc + dw_ref[0, j - 1:j, :] * shifted[SUBLANES * a:SUBLANES * a + tm]
    mu = jnp.mean(acc, axis=-1, keepdims=True)
    cen = acc - mu
    var = jnp.mean(cen * cen, axis=-1, keepdims=True)
    feat = jax.nn.silu(cen * lax.rsqrt(var + EPS) * lng_ref[0] + lnb_ref[0]).astype(BF16)
    y_conv = jnp.dot(feat, wconv_ref[0], preferred_element_type=F32)
    merged = (gates[:, :D_MODEL] * y_conv + gates[:, D_MODEL:] * y_attn).astype(BF16)
    o_ref[0] = x + mod[2:3] * jnp.dot(merged, wout_ref[0], preferred_element_type=F32)


def _mix(x, u, attn, layer, mods, mod_row, w_in, p, tm):
    b, t, d = x.shape
    per = tm // CONV_HALO
    last = t // CONV_HALO - 1
    seq = lambda width: pl.BlockSpec((1, tm, width), lambda bi, i: (bi, i, 0))
    weights = (p["norm1_g"], p["conv_dw"], p["conv_dw_b"], p["conv_ln_g"], p["conv_ln_b"],
               p["w_conv_out"], p["w_na_out"], p["w_out"])
    return pl.pallas_call(
        _mix_body,
        grid=(b, t // tm),
        in_specs=[
            seq(d),
            pl.BlockSpec((1, CONV_HALO, CONV_DIM), lambda bi, i: (bi, jnp.maximum(i * per - 1, 0), 0)),
            seq(CONV_DIM),
            pl.BlockSpec((1, CONV_HALO, CONV_DIM), lambda bi, i: (bi, jnp.minimum((i + 1) * per, last), 0)),
            seq(NA_DIM),
            _mod_spec(layer, mod_row),
            _whole_spec(w_in),
        ] + [_layer_spec(w, layer) for w in weights],
        out_specs=seq(d),
        out_shape=jax.ShapeDtypeStruct((b, t, d), F32),
        compiler_params=_params(("arbitrary", "arbitrary")),
        name="mix",
    )(x, u, u, u, attn, mods, w_in, *weights)


def _ffn_body(final_norm, x_ref, xp_ref, xn_ref, mod_ref, wup_ref, wdown_ref, g_ref, dw_ref, dwb_ref, fg_ref,
              o_ref):
    i = pl.program_id(1)
    tm = x_ref.shape[1]
    mod = mod_ref[0, 0]
    gain = g_ref[0]
    x = x_ref[0]
    hp = _rms_modulate(xp_ref[0], gain, mod[3:4], mod[4:5])
    hn = _rms_modulate(xn_ref[0], gain, mod[3:4], mod[4:5])
    hp = jnp.where(i > 0, hp, jnp.zeros_like(hp))
    hn = jnp.where(i < pl.num_programs(1) - 1, hn, jnp.zeros_like(hn))
    h = jnp.concatenate([hp, _rms_modulate(x, gain, mod[3:4], mod[4:5]), hn], axis=0).astype(BF16)
    n = tm + 2 * FFN_HALO
    assert FFN_CONV_K == 3

    def conv(cols, up):
        before = pltpu.roll(up, 1, 0)[FFN_HALO:FFN_HALO + tm]
        after = pltpu.roll(up, n - 1, 0)[FFN_HALO:FFN_HALO + tm]
        return (dwb_ref[0, :, cols] + dw_ref[0, 0:1, cols] * before
                + dw_ref[0, 1:2, cols] * up[FFN_HALO:FFN_HALO + tm] + dw_ref[0, 2:3, cols] * after)

    value_cols = slice(0, FFN_DIM)
    gate_cols = slice(FFN_DIM, 2 * FFN_DIM)
    up_gate = jnp.dot(h, wup_ref[:, gate_cols], preferred_element_type=F32)
    up_value = jnp.dot(h, wup_ref[:, value_cols], preferred_element_type=F32)
    act = (jax.nn.silu(conv(gate_cols, up_gate)) * conv(value_cols, up_value)).astype(BF16)
    y = x + mod[5:6] * jnp.dot(act, wdown_ref[...], preferred_element_type=F32)
    if final_norm:
        y = y * lax.rsqrt(jnp.mean(y * y, axis=-1, keepdims=True) + EPS) * fg_ref[...]
    o_ref[0] = y


def _ffn(x, layer, mods, mod_row, w_up, w_down, p, final_gain, final_norm, tm):
    b, t, d = x.shape
    per = tm // FFN_HALO
    last = t // FFN_HALO - 1
    seq = pl.BlockSpec((1, tm, d), lambda bi, i: (bi, i, 0))
    weights = (p["norm2_g"], p["ffn_dw"], p["ffn_dw_b"])
    return pl.pallas_call(
        functools.partial(_ffn_body, final_norm),
        grid=(b, t // tm),
        in_specs=[
            seq,
            pl.BlockSpec((1, FFN_HALO, d), lambda bi, i: (bi, jnp.maximum(i * per - 1, 0), 0)),
            pl.BlockSpec((1, FFN_HALO, d), lambda bi, i: (bi, jnp.minimum((i + 1) * per, last), 0)),
            _mod_spec(layer, mod_row),
            _whole_spec(w_up),
            _whole_spec(w_down),
        ] + [_layer_spec(w, layer) for w in weights] + [_whole_spec(final_gain)],
        out_specs=seq,
        out_shape=jax.ShapeDtypeStruct((b, t, d), F32),
        compiler_params=_params(("arbitrary", "arbitrary")),
        name="ffn",
    )(x, x, x, mods, w_up, w_down, *weights, final_gain)


def _rope_tables(seq_len):
    half = HEAD_DIM // 2
    inv_freq = ROPE_THETA ** (-jnp.arange(0, half, 2, dtype=F32) / half)
    t = jnp.arange(seq_len)
    ang_r = (t // GRID_W).astype(F32)[:, None] * inv_freq[None, :]
    ang_c = (t % GRID_W).astype(F32)[:, None] * inv_freq[None, :]
    cos = jnp.concatenate([jnp.cos(ang_r)] * 2 + [jnp.cos(ang_c)] * 2, axis=-1)
    sin = jnp.concatenate([-jnp.sin(ang_r), jnp.sin(ang_r), -jnp.sin(ang_c), jnp.sin(ang_c)], axis=-1)
    return jnp.tile(cos, (1, HEADS_PER_BLOCK)), jnp.tile(sin, (1, HEADS_PER_BLOCK))


def kernel(x, c, ctx, c_ctx, w_ada, b_ada, norm1_g, w_in, conv_dw, conv_dw_b, conv_ln_g, conv_ln_b, w_conv_out,
           na_rpb, w_na_out, w_out, norm2_g, w_up, ffn_dw, ffn_dw_b, w_down, final_norm_g):
    batch, seq_len, d = x.shape
    depth = w_ada.shape[0]
    ctx_len = ctx.shape[1]
    assert d == D_MODEL and seq_len % GRID_W == 0 and batch + 1 <= MOD_ROWS

    cvec = jnp.zeros((MOD_ROWS, d), F32).at[:batch].set(c).at[batch].set(c_ctx)
    mods = _ada(cvec, w_ada, b_ada).reshape(depth, MOD_ROWS, N_MOD, d)
    lat_row = lambda bi: bi
    ctx_row = lambda bi: batch

    cos, sin = _rope_tables(seq_len)
    cos_id = jnp.ones((ctx_len, LANES), F32)
    sin_id = jnp.zeros((ctx_len, LANES), F32)
    rows = lambda v: v.reshape(depth, 1, -1)
    p = {
        "norm1_g": rows(norm1_g), "norm2_g": rows(norm2_g),
        "conv_dw": conv_dw, "conv_dw_b": rows(conv_dw_b), "conv_ln_g": rows(conv_ln_g), "conv_ln_b": rows(conv_ln_b),
        "ffn_dw": ffn_dw, "ffn_dw_b": rows(ffn_dw_b),
        "w_conv_out": w_conv_out.astype(BF16), "w_na_out": w_na_out.astype(BF16), "w_out": w_out.astype(BF16),
    }
    fin = final_norm_g.reshape(1, d)
    bias = _window_bias(na_rpb)

    xc = ctx
    for l in range(depth):
        last = l == depth - 1
        u_c, q_c, k_c, v_c, w_in_l = _inproj(xc, l, mods, ctx_row, p["norm1_g"], cos_id, sin_id, w_in, ctx_len)
        u, q, k, v = _inproj(x, l, mods, lat_row, p["norm1_g"], cos, sin, w_in_l, INPROJ_TM)
        attn, w_up_l, w_down_l = _neighbourhood_attention(q, k, v, k_c, v_c, bias, w_up, w_down, l)
        x = _mix(x, u, attn, l, mods, lat_row, w_in_l, p, MIX_TM)
        x = _ffn(x, l, mods, lat_row, w_up_l, w_down_l, p, fin, last, FFN_TM)
        if not last:
            attn_c = _context_attention(q_c, k_c, v_c)
            xc = _mix(xc, u_c, attn_c, l, mods, ctx_row, w_in_l, p, ctx_len)
            xc = _ffn(xc, l, mods, ctx_row, w_up_l, w_down_l, p, fin, False, ctx_len)
    return x
```

```python
import functools

import jax
import jax.numpy as jnp
from jax import lax
from jax.experimental import pallas as pl
from jax.experimental.pallas import tpu as pltpu

D_MODEL = 1024
GRID_W = 64
CONV_DIM = 512
CONV_K = 31
NA_HEADS = 8
HEAD_DIM = 64
NA_DIM = NA_HEADS * HEAD_DIM
NA_KH = 8
NA_KW = 16
ROPE_THETA = 10000.0
FFN_DIM = 2816
FFN_CONV_K = 3
EPS = 1e-6
GLU_END = 2 * CONV_DIM
QKV_END = GLU_END + 3 * NA_DIM

LANES = 128
SUBLANES = 8
BF16_ROWS = 16
V7X_VMEM_BYTES = 64 * 1024 * 1024
VMEM_LIMIT = V7X_VMEM_BYTES * 7 // 8

HEADS_PER_BLOCK = LANES // HEAD_DIM
N_HEAD_BLOCKS = NA_HEADS // HEADS_PER_BLOCK
CONV_HALO = BF16_ROWS
FFN_HALO = SUBLANES
MOD_ROWS = 8
N_MOD = 6
NA_ROW_GROUP = 8
NA_SLAB = 32

INPROJ_TM = 1024
MIX_TM = 512
FFN_TM = 512
ADA_TK = 256

BF16 = jnp.bfloat16
F32 = jnp.float32


def _params(semantics):
    return pltpu.CompilerParams(dimension_semantics=semantics, vmem_limit_bytes=VMEM_LIMIT)


def _layer_spec(arr, layer):
    tail = (0,) * (arr.ndim - 1)
    return pl.BlockSpec((1,) + arr.shape[1:], lambda *_: (layer,) + tail, pipeline_mode=pl.Buffered(1))


def _whole_spec(arr):
    zeros = (0,) * arr.ndim
    return pl.BlockSpec(arr.shape, lambda *_: zeros, pipeline_mode=pl.Buffered(1))


def _mod_spec(layer, mod_row):
    return pl.BlockSpec((1, 1, N_MOD, D_MODEL), lambda bi, i: (layer, mod_row(bi), 0, 0))


def _rms_modulate(x, gain, shift, scale):
    return x * lax.rsqrt(jnp.mean(x * x, axis=-1, keepdims=True) + EPS) * (gain * (1.0 + scale)) + shift


def _ada_body(c_ref, w_ref, b_ref, o_ref):
    s = jax.nn.silu(c_ref[...]).astype(BF16)
    part = jnp.dot(s, w_ref[0].astype(BF16), preferred_element_type=F32)

    @pl.when(pl.program_id(1) == 0)
    def _():
        o_ref[0] = part + b_ref[0]

    @pl.when(pl.program_id(1) > 0)
    def _():
        o_ref[0] += part


def _ada_bias_body(c_ref, w_ref, b_ref, rpb_ref, o_ref, bias_ref):
    _ada_body(c_ref, w_ref, b_ref, o_ref)
    _bias_body(rpb_ref, bias_ref)


def _ada_and_window_bias(cvec, w_ada, b_ada, rpb):
    depth, d, n = w_ada.shape
    assert d // ADA_TK == N_HEAD_BLOCKS
    tile = (HEADS_PER_BLOCK * GRID_W, NA_KH * GRID_W)
    return pl.pallas_call(
        _ada_bias_body,
        grid=(depth, N_HEAD_BLOCKS),
        in_specs=[
            pl.BlockSpec((MOD_ROWS, ADA_TK), lambda l, k: (0, k)),
            pl.BlockSpec((1, ADA_TK, n), lambda l, k: (l, k, 0)),
            pl.BlockSpec((1, 1, n), lambda l, k: (l, 0, 0)),
            pl.BlockSpec(memory_space=pltpu.SMEM),
        ],
        out_specs=[pl.BlockSpec((1, MOD_ROWS, n), lambda l, k: (l, 0, 0)),
                   pl.BlockSpec((1, NA_KH, 1) + tile, lambda l, hb: (l, 0, hb, 0, 0))],
        out_shape=[jax.ShapeDtypeStruct((depth, MOD_ROWS, n), F32),
                   jax.ShapeDtypeStruct((depth, NA_KH, N_HEAD_BLOCKS) + tile, F32)],
        compiler_params=_params(("arbitrary", "arbitrary")),
        name="ada_bias",
    )(cvec, w_ada, b_ada.reshape(depth, 1, n), rpb.reshape(-1))


def _bias_body(rpb_ref, o_ref):
    layer = pl.program_id(0)
    hb = pl.program_id(1)
    c = lax.broadcasted_iota(jnp.int32, (GRID_W, GRID_W), 0)
    cp = lax.broadcasted_iota(jnp.int32, (GRID_W, GRID_W), 1)
    start = jnp.clip(c - NA_KW // 2, 0, GRID_W - NA_KW)
    inside = (cp >= start) & (cp < start + NA_KW)
    diff = cp - c + (NA_KW - 1)
    n_dr = 2 * NA_KH - 1
    n_dc = 2 * NA_KW - 1
    for hh in range(HEADS_PER_BLOCK):
        h = (layer * N_HEAD_BLOCKS + hb) * HEADS_PER_BLOCK + hh
        for i in range(n_dr):
            val = jnp.zeros((GRID_W, GRID_W), F32)
            for d in range(n_dc):
                val = jnp.where(diff == d, rpb_ref[(h * n_dr + i) * n_dc + d], val)
            val = jnp.where(inside, val, -jnp.inf)
            for v in range(NA_KH):
                j = i - v
                if 0 <= j < NA_KH:
                    o_ref[0, v, 0, hh * GRID_W:(hh + 1) * GRID_W, j * GRID_W:(j + 1) * GRID_W] = val


def _rope(t, cos, sin, swap_up):
    partner = jnp.where(swap_up, pltpu.roll(t, LANES - HEAD_DIM // 4, 1), pltpu.roll(t, HEAD_DIM // 4, 1))
    return t * cos + partner * sin


def _inproj_body(cast_weights, x_ref, mod_ref, g_ref, cos_ref, sin_ref, win_ref, u_ref, q_ref, k_ref, v_ref,
                 *wb_refs):
    if cast_weights:
        (w_ref,) = wb_refs

        @pl.when((pl.program_id(0) == 0) & (pl.program_id(1) == 0))
        def _():
            w_ref[...] = win_ref[0].astype(BF16)
    else:
        w_ref = win_ref
    mod = mod_ref[0, 0]
    h = _rms_modulate(x_ref[0], g_ref[0], mod[0:1], mod[1:2]).astype(BF16)
    pg = jnp.dot(h, w_ref[:, :GLU_END], preferred_element_type=F32)
    u_ref[0] = (pg[:, :CONV_DIM] * jax.nn.sigmoid(pg[:, CONV_DIM:])).astype(BF16)
    pq = jnp.dot(h, w_ref[:, GLU_END:QKV_END], preferred_element_type=F32)
    cos = cos_ref[...]
    sin = sin_ref[...]
    lane = lax.broadcasted_iota(jnp.int32, cos.shape, 1)
    swap_up = (lane % (HEAD_DIM // 2)) < (HEAD_DIM // 4)
    for blk in range(N_HEAD_BLOCKS):
        sl = slice(blk * LANES, (blk + 1) * LANES)
        q_ref[0, :, sl] = (_rope(pq[:, sl], cos, sin, swap_up) * (HEAD_DIM ** -0.5)).astype(BF16)
        ks = slice(NA_DIM + blk * LANES, NA_DIM + (blk + 1) * LANES)
        k_ref[0, :, sl] = _rope(pq[:, ks], cos, sin, swap_up).astype(BF16)
    v_ref[0] = pq[:, 2 * NA_DIM:].astype(BF16)


def _inproj(x, layer, mods, mod_row, gain, cos, sin, w_in, tm):
    b, t, d = x.shape
    cast_weights = w_in.ndim == 3
    seq_spec = lambda width: pl.BlockSpec((1, tm, width), lambda bi, i: (bi, i, 0))
    out = jax.ShapeDtypeStruct((b, t, NA_DIM), BF16)
    out_specs = [seq_spec(CONV_DIM), seq_spec(NA_DIM), seq_spec(NA_DIM), seq_spec(NA_DIM)]
    out_shape = [jax.ShapeDtypeStruct((b, t, CONV_DIM), BF16), out, out, out]
    if cast_weights:
        out_specs.append(pl.BlockSpec(w_in.shape[1:], lambda bi, i: (0, 0)))
        out_shape.append(jax.ShapeDtypeStruct(w_in.shape[1:], BF16))
    return pl.pallas_call(
        functools.partial(_inproj_body, cast_weights),
        grid=(b, t // tm),
        in_specs=[
            seq_spec(d),
            _mod_spec(layer, mod_row),
            _layer_spec(gain, layer),
            pl.BlockSpec((tm, LANES), lambda bi, i: (i, 0)),
            pl.BlockSpec((tm, LANES), lambda bi, i: (i, 0)),
            _layer_spec(w_in, layer) if cast_weights else _whole_spec(w_in),
        ],
        out_specs=out_specs,
        out_shape=out_shape,
        compiler_params=_params(("arbitrary", "arbitrary")),
        name="inproj",
    )(x, mods, gain, cos, sin, w_in)


def _stack_heads(t):
    lane = lax.broadcasted_iota(jnp.int32, t.shape, 1)
    zero = jnp.zeros_like(t)
    return jnp.concatenate([jnp.where(lane < HEAD_DIM, t, zero), jnp.where(lane >= HEAD_DIM, t, zero)], axis=0)


def _unstack_heads(o, rows):
    lane = lax.broadcasted_iota(jnp.int32, (rows, LANES), 1)
    return jnp.where(lane < HEAD_DIM, o[:rows], o[rows:])


def _scores(lhs, keys):
    return lax.dot_general(lhs, keys, (((1,), (1,)), ((), ())), preferred_element_type=F32)


def _na_body(q_ref, k_ref, v_ref, kc_ref, vc_ref, bias_ref, wup_ref, wdown_ref, o_ref, wup_bf_ref, wdown_bf_ref):
    wup_bf_ref[...] = wup_ref[0].astype(BF16)
    wdown_bf_ref[...] = wdown_ref[0].astype(BF16)
    rows = q_ref.shape[1] // GRID_W
    band = NA_KH * GRID_W
    kc = kc_ref[0]
    vc = vc_ref[0]

    def group(g, carry):
        rs = [g * NA_ROW_GROUP + j for j in range(NA_ROW_GROUP)]
        starts = [jnp.clip(r - NA_KH // 2, 0, rows - NA_KH) for r in rs]
        q0 = [pl.multiple_of(r * GRID_W, GRID_W) for r in rs]
        k0 = [pl.multiple_of(s * GRID_W, GRID_W) for s in starts]
        lhs = [_stack_heads(q_ref[0, pl.ds(a, GRID_W), :]) for a in q0]
        variant = [s - r + (NA_KH - 1) for s, r in zip(starts, rs)]
        s_win = [_scores(l, k_ref[0, pl.ds(a, band), :]) for l, a in zip(lhs, k0)]
        s_ctx = [_scores(l, kc) for l in lhs]
        n_slabs = 2 * GRID_W // NA_SLAB
        for j in range(NA_ROW_GROUP):
            pw, pc, inv = [], [], []
            for t in range(n_slabs):
                sl = slice(t * NA_SLAB, (t + 1) * NA_SLAB)
                sw = s_win[j][sl] + bias_ref[0, variant[j], 0, sl, :]
                sc = s_ctx[j][sl]
                m = jnp.maximum(jnp.max(sw, axis=-1, keepdims=True), jnp.max(sc, axis=-1, keepdims=True))
                ew = jnp.exp(sw - m)
                ec = jnp.exp(sc - m)
                inv.append(1.0 / (jnp.sum(ew, axis=-1, keepdims=True) + jnp.sum(ec, axis=-1, keepdims=True)))
                pw.append(ew.astype(BF16))
                pc.append(ec.astype(BF16))
            o = jnp.dot(jnp.concatenate(pw, axis=0), v_ref[0, pl.ds(k0[j], band), :], preferred_element_type=F32)
            o = o + jnp.dot(jnp.concatenate(pc, axis=0), vc, preferred_element_type=F32)
            o = o * jnp.concatenate(inv, axis=0)
            o_ref[0, pl.ds(q0[j], GRID_W), :] = _unstack_heads(o, GRID_W).astype(BF16)
        return carry

    lax.fori_loop(0, rows // NA_ROW_GROUP, group, 0)


def _neighbourhood_attention(q, k, v, kc, vc, bias, w_up, w_down, layer):
    b, t, _ = q.shape
    ctx = kc.shape[1]
    steps = b * N_HEAD_BLOCKS
    assert (t // GRID_W) % NA_ROW_GROUP == 0
    seq = lambda length: pl.BlockSpec((1, length, LANES), lambda bi, hb: (bi, 0, hb))

    def slab_specs(w):
        rows, cols = w.shape[1:]
        assert rows % (steps * BF16_ROWS) == 0
        slab = rows // steps
        return (pl.BlockSpec((1, slab, cols), lambda bi, hb: (layer, bi * N_HEAD_BLOCKS + hb, 0)),
                pl.BlockSpec((slab, cols), lambda bi, hb: (bi * N_HEAD_BLOCKS + hb, 0)),
                jax.ShapeDtypeStruct((rows, cols), BF16))

    up_in, up_out, up_shape = slab_specs(w_up)
    down_in, down_out, down_shape = slab_specs(w_down)
    return pl.pallas_call(
        _na_body,
        grid=(b, N_HEAD_BLOCKS),
        in_specs=[seq(t), seq(t), seq(t), seq(ctx), seq(ctx),
                  pl.BlockSpec((1, NA_KH, 1) + bias.shape[3:], lambda bi, hb: (layer, 0, hb, 0, 0)),
                  up_in, down_in],
        out_specs=[seq(t), up_out, down_out],
        out_shape=[jax.ShapeDtypeStruct((b, t, NA_DIM), BF16), up_shape, down_shape],
        compiler_params=_params(("arbitrary", "arbitrary")),
        name="na_attention",
    )(q, k, v, kc, vc, bias, w_up, w_down)


def _ctx_attn_body(q_ref, k_ref, v_ref, o_ref):
    n = q_ref.shape[1]
    lhs = _stack_heads(q_ref[0])
    s = _scores(lhs, k_ref[0])
    p = jnp.exp(s - jnp.max(s, axis=-1, keepdims=True))
    denom = jnp.sum(p, axis=-1, keepdims=True)
    o = jnp.dot(p.astype(BF16), v_ref[0], preferred_element_type=F32)
    o_ref[0] = _unstack_heads(o / denom, n).astype(BF16)


def _context_attention(q, k, v):
    b, n, _ = q.shape
    spec = pl.BlockSpec((1, n, LANES), lambda bi, hb: (bi, 0, hb))
    return pl.pallas_call(
        _ctx_attn_body,
        grid=(b, N_HEAD_BLOCKS),
        in_specs=[spec, spec, spec],
        out_specs=spec,
        out_shape=jax.ShapeDtypeStruct((b, n, NA_DIM), BF16),
        compiler_params=_params(("arbitrary", "arbitrary")),
        name="ctx_attention",
    )(q, k, v)


def _mix_body(x_ref, up_ref, uc_ref, un_ref, a_ref, mod_ref, win_ref, g_ref, dw_ref, dwb_ref, lng_ref, lnb_ref,
              wconv_ref, wna_ref, wout_ref, o_ref):
    i = pl.program_id(1)
    tm = x_ref.shape[1]
    x = x_ref[0]
    mod = mod_ref[0, 0]
    h = _rms_modulate(x, g_ref[0], mod[0:1], mod[1:2]).astype(BF16)
    gates = jax.nn.sigmoid(jnp.dot(h, win_ref[:, QKV_END:], preferred_element_type=F32))
    y_attn = jnp.dot(a_ref[0], wna_ref[0], preferred_element_type=F32)
    prev = up_ref[0].astype(F32)
    nxt = un_ref[0].astype(F32)
    prev = jnp.where(i > 0, prev, jnp.zeros_like(prev))
    nxt = jnp.where(i < pl.num_programs(1) - 1, nxt, jnp.zeros_like(nxt))
    ucat = jnp.concatenate([prev, uc_ref[0].astype(F32), nxt], axis=0)
    n = tm + 2 * CONV_HALO
    assert CONV_HALO - CONV_K // 2 == 1 and CONV_K < 4 * SUBLANES
    acc = jnp.zeros((tm, CONV_DIM), F32) + dwb_ref[0]
    for b in range(SUBLANES):
        shifted = ucat if b == 0 else pltpu.roll(ucat, n - b, 0)
        for a in range(4):
            j = SUBLANES * a + b
            if 1 <= j <= CONV_K:
                acc = acc + dw_ref[0, j - 1:j, :] * shifted[SUBLANES * a:SUBLANES * a + tm]
    mu = jnp.mean(acc, axis=-1, keepdims=True)
    cen = acc - mu
    var = jnp.mean(cen * cen, axis=-1, keepdims=True)
    feat = jax.nn.silu(cen * lax.rsqrt(var + EPS) * lng_ref[0] + lnb_ref[0]).astype(BF16)
    y_conv = jnp.dot(feat, wconv_ref[0], preferred_element_type=F32)
    merged = (gates[:, :D_MODEL] * y_conv + gates[:, D_MODEL:] * y_attn).astype(BF16)
    o_ref[0] = x + mod[2:3] * jnp.dot(merged, wout_ref[0], preferred_element_type=F32)


def _mix(x, u, attn, layer, mods, mod_row, w_in, p, tm):
    b, t, d = x.shape
    per = tm // CONV_HALO
    last = t // CONV_HALO - 1
    seq = lambda width: pl.BlockSpec((1, tm, width), lambda bi, i: (bi, i, 0))
    weights = (p["norm1_g"], p["conv_dw"], p["conv_dw_b"], p["conv_ln_g"], p["conv_ln_b"],
               p["w_conv_out"], p["w_na_out"], p["w_out"])
    return pl.pallas_call(
        _mix_body,
        grid=(b, t // tm),
        in_specs=[
            seq(d),
            pl.BlockSpec((1, CONV_HALO, CONV_DIM), lambda bi, i: (bi, jnp.maximum(i * per - 1, 0), 0)),
            seq(CONV_DIM),
            pl.BlockSpec((1, CONV_HALO, CONV_DIM), lambda bi, i: (bi, jnp.minimum((i + 1) * per, last), 0)),
            seq(NA_DIM),
            _mod_spec(layer, mod_row),
            _whole_spec(w_in),
        ] + [_layer_spec(w, layer) for w in weights],
        out_specs=seq(d),
        out_shape=jax.ShapeDtypeStruct((b, t, d), F32),
        compiler_params=_params(("arbitrary", "arbitrary")),
        name="mix",
    )(x, u, u, u, attn, mods, w_in, *weights)


def _ffn_body(final_norm, x_ref, xp_ref, xn_ref, mod_ref, wup_ref, wdown_ref, g_ref, dw_ref, dwb_ref, fg_ref,
              o_ref):
    i = pl.program_id(1)
    tm = x_ref.shape[1]
    mod = mod_ref[0, 0]
    gain = g_ref[0]
    x = x_ref[0]
    hp = _rms_modulate(xp_ref[0], gain, mod[3:4], mod[4:5])
    hn = _rms_modulate(xn_ref[0], gain, mod[3:4], mod[4:5])
    hp = jnp.where(i > 0, hp, jnp.zeros_like(hp))
    hn = jnp.where(i < pl.num_programs(1) - 1, hn, jnp.zeros_like(hn))
    h = jnp.concatenate([hp, _rms_modulate(x, gain, mod[3:4], mod[4:5]), hn], axis=0).astype(BF16)
    n = tm + 2 * FFN_HALO
    assert FFN_CONV_K == 3

    def conv(cols, up):
        before = pltpu.roll(up, 1, 0)[FFN_HALO:FFN_HALO + tm]
        after = pltpu.roll(up, n - 1, 0)[FFN_HALO:FFN_HALO + tm]
        return (dwb_ref[0, :, cols] + dw_ref[0, 0:1, cols] * before
                + dw_ref[0, 1:2, cols] * up[FFN_HALO:FFN_HALO + tm] + dw_ref[0, 2:3, cols] * after)

    value_cols = slice(0, FFN_DIM)
    gate_cols = slice(FFN_DIM, 2 * FFN_DIM)
    up_gate = jnp.dot(h, wup_ref[:, gate_cols], preferred_element_type=F32)
    up_value = jnp.dot(h, wup_ref[:, value_cols], preferred_element_type=F32)
    act = (jax.nn.silu(conv(gate_cols, up_gate)) * conv(value_cols, up_value)).astype(BF16)
    y = x + mod[5:6] * jnp.dot(act, wdown_ref[...], preferred_element_type=F32)
    if final_norm:
        y = y * lax.rsqrt(jnp.mean(y * y, axis=-1, keepdims=True) + EPS) * fg_ref[...]
    o_ref[0] = y


def _ffn(x, layer, mods, mod_row, w_up, w_down, p, final_gain, final_norm, tm):
    b, t, d = x.shape
    per = tm // FFN_HALO
    last = t // FFN_HALO - 1
    seq = pl.BlockSpec((1, tm, d), lambda bi, i: (bi, i, 0))
    weights = (p["norm2_g"], p["ffn_dw"], p["ffn_dw_b"])
    return pl.pallas_call(
        functools.partial(_ffn_body, final_norm),
        grid=(b, t // tm),
        in_specs=[
            seq,
            pl.BlockSpec((1, FFN_HALO, d), lambda bi, i: (bi, jnp.maximum(i * per - 1, 0), 0)),
            pl.BlockSpec((1, FFN_HALO, d), lambda bi, i: (bi, jnp.minimum((i + 1) * per, last), 0)),
            _mod_spec(layer, mod_row),
            _whole_spec(w_up),
            _whole_spec(w_down),
        ] + [_layer_spec(w, layer) for w in weights] + [_whole_spec(final_gain)],
        out_specs=seq,
        out_shape=jax.ShapeDtypeStruct((b, t, d), F32),
        compiler_params=_params(("arbitrary", "arbitrary")),
        name="ffn",
    )(x, x, x, mods, w_up, w_down, *weights, final_gain)


def _rope_tables(seq_len):
    half = HEAD_DIM // 2
    inv_freq = ROPE_THETA ** (-jnp.arange(0, half, 2, dtype=F32) / half)
    t = jnp.arange(seq_len)
    ang_r = (t // GRID_W).astype(F32)[:, None] * inv_freq[None, :]
    ang_c = (t % GRID_W).astype(F32)[:, None] * inv_freq[None, :]
    cos = jnp.concatenate([jnp.cos(ang_r)] * 2 + [jnp.cos(ang_c)] * 2, axis=-1)
    sin = jnp.concatenate([-jnp.sin(ang_r), jnp.sin(ang_r), -jnp.sin(ang_c), jnp.sin(ang_c)], axis=-1)
    return jnp.tile(cos, (1, HEADS_PER_BLOCK)), jnp.tile(sin, (1, HEADS_PER_BLOCK))


def kernel(x, c, ctx, c_ctx, w_ada, b_ada, norm1_g, w_in, conv_dw, conv_dw_b, conv_ln_g, conv_ln_b, w_conv_out,
           na_rpb, w_na_out, w_out, norm2_g, w_up, ffn_dw, ffn_dw_b, w_down, final_norm_g):
    batch, seq_len, d = x.shape
    depth = w_ada.shape[0]
    ctx_len = ctx.shape[1]
    assert d == D_MODEL and seq_len % GRID_W == 0 and batch + 1 <= MOD_ROWS

    cvec = jnp.zeros((MOD_ROWS, d), F32).at[:batch].set(c).at[batch].set(c_ctx)
    mods, bias = _ada_and_window_bias(cvec, w_ada, b_ada, na_rpb)
    mods = mods.reshape(depth, MOD_ROWS, N_MOD, d)
    lat_row = lambda bi: bi
    ctx_row = lambda bi: batch

    cos, sin = _rope_tables(seq_len)
    cos_id = jnp.ones((ctx_len, LANES), F32)
    sin_id = jnp.zeros((ctx_len, LANES), F32)
    rows = lambda v: v.reshape(depth, 1, -1)
    p = {
        "norm1_g": rows(norm1_g), "norm2_g": rows(norm2_g),
        "conv_dw": conv_dw, "conv_dw_b": rows(conv_dw_b), "conv_ln_g": rows(conv_ln_g), "conv_ln_b": rows(conv_ln_b),
        "ffn_dw": ffn_dw, "ffn_dw_b": rows(ffn_dw_b),
        "w_conv_out": w_conv_out.astype(BF16), "w_na_out": w_na_out.astype(BF16), "w_out": w_out.astype(BF16),
    }
    fin = final_norm_g.reshape(1, d)

    xc = ctx
    for l in range(depth):
        last = l == depth - 1
        u_c, q_c, k_c, v_c, w_in_l = _inproj(xc, l, mods, ctx_row, p["norm1_g"], cos_id, sin_id, w_in, ctx_len)
        u, q, k, v = _inproj(x, l, mods, lat_row, p["norm1_g"], cos, sin, w_in_l, INPROJ_TM)
        attn, w_up_l, w_down_l = _neighbourhood_attention(q, k, v, k_c, v_c, bias, w_up, w_down, l)
        x = _mix(x, u, attn, l, mods, lat_row, w_in_l, p, MIX_TM)
        x = _ffn(x, l, mods, lat_row, w_up_l, w_down_l, p, fin, last, FFN_TM)
        if not last:
            attn_c = _context_attention(q_c, k_c, v_c)
            xc = _mix(xc, u_c, attn_c, l, mods, ctx_row, w_in_l, p, ctx_len)
            xc = _ffn(xc, l, mods, ctx_row, w_up_l, w_down_l, p, fin, False, ctx_len)
    return x
```

```python
import functools

import jax
import jax.numpy as jnp
from jax import lax
from jax.experimental import pallas as pl
from jax.experimental.pallas import tpu as pltpu

D_MODEL = 1024
GRID_W = 64
CONV_DIM = 512
CONV_K = 31
NA_HEADS = 8
HEAD_DIM = 64
NA_DIM = NA_HEADS * HEAD_DIM
NA_KH = 8
NA_KW = 16
ROPE_THETA = 10000.0
FFN_DIM = 2816
FFN_CONV_K = 3
EPS = 1e-6
GLU_END = 2 * CONV_DIM
QKV_END = GLU_END + 3 * NA_DIM

LANES = 128
SUBLANES = 8
BF16_ROWS = 16
V7X_VMEM_BYTES = 64 * 1024 * 1024
VMEM_LIMIT = V7X_VMEM_BYTES * 7 // 8

HEADS_PER_BLOCK = LANES // HEAD_DIM
N_HEAD_BLOCKS = NA_HEADS // HEADS_PER_BLOCK
CONV_HALO = BF16_ROWS
FFN_HALO = SUBLANES
MOD_ROWS = 8
N_MOD = 6
NA_ROW_GROUP = 8
NA_SLAB = 32

INPROJ_TM = 1024
MIX_TM = 1024
FFN_TM = 512
ADA_TK = 256

BF16 = jnp.bfloat16
F32 = jnp.float32


def _params(semantics):
    return pltpu.CompilerParams(dimension_semantics=semantics, vmem_limit_bytes=VMEM_LIMIT)


def _layer_spec(arr, layer):
    tail = (0,) * (arr.ndim - 1)
    return pl.BlockSpec((1,) + arr.shape[1:], lambda *_: (layer,) + tail, pipeline_mode=pl.Buffered(1))


def _whole_spec(arr):
    zeros = (0,) * arr.ndim
    return pl.BlockSpec(arr.shape, lambda *_: zeros, pipeline_mode=pl.Buffered(1))


def _mod_spec(layer, mod_row):
    return pl.BlockSpec((1, 1, N_MOD, D_MODEL), lambda bi, i: (layer, mod_row(bi), 0, 0))


def _rms_modulate(x, gain, shift, scale):
    return x * lax.rsqrt(jnp.mean(x * x, axis=-1, keepdims=True) + EPS) * (gain * (1.0 + scale)) + shift


def _ada_body(c_ref, w_ref, b_ref, o_ref):
    s = jax.nn.silu(c_ref[...]).astype(BF16)
    part = jnp.dot(s, w_ref[0].astype(BF16), preferred_element_type=F32)

    @pl.when(pl.program_id(1) == 0)
    def _():
        o_ref[0] = part + b_ref[0]

    @pl.when(pl.program_id(1) > 0)
    def _():
        o_ref[0] += part


def _ada_bias_body(n_cast, c_ref, w_ref, b_ref, rpb_ref, *refs):
    f32_refs, (o_ref, bias_ref), bf16_refs = refs[:n_cast], refs[n_cast:n_cast + 2], refs[n_cast + 2:]
    _ada_body(c_ref, w_ref, b_ref, o_ref)
    _bias_body(rpb_ref, bias_ref)
    for src, dst in zip(f32_refs, bf16_refs):
        dst[...] = src[...].astype(BF16)


def _ada_and_window_bias(cvec, w_ada, b_ada, rpb, cast_weights):
    depth, d, n = w_ada.shape
    steps = d // ADA_TK
    assert steps == N_HEAD_BLOCKS
    tile = (HEADS_PER_BLOCK * GRID_W, NA_KH * GRID_W)

    def slab(w):
        rows, cols = w.shape[1:]
        assert rows % (steps * BF16_ROWS) == 0
        return pl.BlockSpec((1, rows // steps, cols), lambda l, k: (l, k, 0))

    outs = pl.pallas_call(
        functools.partial(_ada_bias_body, len(cast_weights)),
        grid=(depth, steps),
        in_specs=[
            pl.BlockSpec((MOD_ROWS, ADA_TK), lambda l, k: (0, k)),
            pl.BlockSpec((1, ADA_TK, n), lambda l, k: (l, k, 0)),
            pl.BlockSpec((1, 1, n), lambda l, k: (l, 0, 0)),
            pl.BlockSpec(memory_space=pltpu.SMEM),
        ] + [slab(w) for w in cast_weights],
        out_specs=[pl.BlockSpec((1, MOD_ROWS, n), lambda l, k: (l, 0, 0)),
                   pl.BlockSpec((1, NA_KH, 1) + tile, lambda l, hb: (l, 0, hb, 0, 0))]
        + [slab(w) for w in cast_weights],
        out_shape=[jax.ShapeDtypeStruct((depth, MOD_ROWS, n), F32),
                   jax.ShapeDtypeStruct((depth, NA_KH, N_HEAD_BLOCKS) + tile, F32)]
        + [jax.ShapeDtypeStruct(w.shape, BF16) for w in cast_weights],
        compiler_params=_params(("arbitrary", "arbitrary")),
        name="ada_bias",
    )(cvec, w_ada, b_ada.reshape(depth, 1, n), rpb.reshape(-1), *cast_weights)
    return outs[0], outs[1], outs[2:]


def _bias_body(rpb_ref, o_ref):
    layer = pl.program_id(0)
    hb = pl.program_id(1)
    c = lax.broadcasted_iota(jnp.int32, (GRID_W, GRID_W), 0)
    cp = lax.broadcasted_iota(jnp.int32, (GRID_W, GRID_W), 1)
    start = jnp.clip(c - NA_KW // 2, 0, GRID_W - NA_KW)
    inside = (cp >= start) & (cp < start + NA_KW)
    diff = cp - c + (NA_KW - 1)
    n_dr = 2 * NA_KH - 1
    n_dc = 2 * NA_KW - 1
    for hh in range(HEADS_PER_BLOCK):
        h = (layer * N_HEAD_BLOCKS + hb) * HEADS_PER_BLOCK + hh
        for i in range(n_dr):
            val = jnp.zeros((GRID_W, GRID_W), F32)
            for d in range(n_dc):
                val = jnp.where(diff == d, rpb_ref[(h * n_dr + i) * n_dc + d], val)
            val = jnp.where(inside, val, -jnp.inf)
            for v in range(NA_KH):
                j = i - v
                if 0 <= j < NA_KH:
                    o_ref[0, v, 0, hh * GRID_W:(hh + 1) * GRID_W, j * GRID_W:(j + 1) * GRID_W] = val


def _rope(t, cos, sin, swap_up):
    partner = jnp.where(swap_up, pltpu.roll(t, LANES - HEAD_DIM // 4, 1), pltpu.roll(t, HEAD_DIM // 4, 1))
    return t * cos + partner * sin


def _inproj_body(cast_weights, x_ref, mod_ref, g_ref, cos_ref, sin_ref, win_ref, u_ref, q_ref, k_ref, v_ref,
                 *wb_refs):
    if cast_weights:
        (w_ref,) = wb_refs

        @pl.when((pl.program_id(0) == 0) & (pl.program_id(1) == 0))
        def _():
            w_ref[...] = win_ref[0].astype(BF16)
    else:
        w_ref = win_ref
    mod = mod_ref[0, 0]
    h = _rms_modulate(x_ref[0], g_ref[0], mod[0:1], mod[1:2]).astype(BF16)
    pg = jnp.dot(h, w_ref[:, :GLU_END], preferred_element_type=F32)
    u_ref[0] = (pg[:, :CONV_DIM] * jax.nn.sigmoid(pg[:, CONV_DIM:])).astype(BF16)
    pq = jnp.dot(h, w_ref[:, GLU_END:QKV_END], preferred_element_type=F32)
    cos = cos_ref[...]
    sin = sin_ref[...]
    lane = lax.broadcasted_iota(jnp.int32, cos.shape, 1)
    swap_up = (lane % (HEAD_DIM // 2)) < (HEAD_DIM // 4)
    for blk in range(N_HEAD_BLOCKS):
        sl = slice(blk * LANES, (blk + 1) * LANES)
        q_ref[0, :, sl] = (_rope(pq[:, sl], cos, sin, swap_up) * (HEAD_DIM ** -0.5)).astype(BF16)
        ks = slice(NA_DIM + blk * LANES, NA_DIM + (blk + 1) * LANES)
        k_ref[0, :, sl] = _rope(pq[:, ks], cos, sin, swap_up).astype(BF16)
    v_ref[0] = pq[:, 2 * NA_DIM:].astype(BF16)


def _inproj(x, layer, mods, mod_row, gain, cos, sin, w_in, tm):
    b, t, d = x.shape
    cast_weights = w_in.ndim == 3
    seq_spec = lambda width: pl.BlockSpec((1, tm, width), lambda bi, i: (bi, i, 0))
    out = jax.ShapeDtypeStruct((b, t, NA_DIM), BF16)
    out_specs = [seq_spec(CONV_DIM), seq_spec(NA_DIM), seq_spec(NA_DIM), seq_spec(NA_DIM)]
    out_shape = [jax.ShapeDtypeStruct((b, t, CONV_DIM), BF16), out, out, out]
    if cast_weights:
        out_specs.append(pl.BlockSpec(w_in.shape[1:], lambda bi, i: (0, 0)))
        out_shape.append(jax.ShapeDtypeStruct(w_in.shape[1:], BF16))
    return pl.pallas_call(
        functools.partial(_inproj_body, cast_weights),
        grid=(b, t // tm),
        in_specs=[
            seq_spec(d),
            _mod_spec(layer, mod_row),
            _layer_spec(gain, layer),
            pl.BlockSpec((tm, LANES), lambda bi, i: (i, 0)),
            pl.BlockSpec((tm, LANES), lambda bi, i: (i, 0)),
            _layer_spec(w_in, layer) if cast_weights else _whole_spec(w_in),
        ],
        out_specs=out_specs,
        out_shape=out_shape,
        compiler_params=_params(("arbitrary", "arbitrary")),
        name="inproj",
    )(x, mods, gain, cos, sin, w_in)


def _stack_heads(t):
    lane = lax.broadcasted_iota(jnp.int32, t.shape, 1)
    zero = jnp.zeros_like(t)
    return jnp.concatenate([jnp.where(lane < HEAD_DIM, t, zero), jnp.where(lane >= HEAD_DIM, t, zero)], axis=0)


def _unstack_heads(o, rows):
    lane = lax.broadcasted_iota(jnp.int32, (rows, LANES), 1)
    return jnp.where(lane < HEAD_DIM, o[:rows], o[rows:])


def _scores(lhs, keys):
    return lax.dot_general(lhs, keys, (((1,), (1,)), ((), ())), preferred_element_type=F32)


def _na_body(q_ref, k_ref, v_ref, kc_ref, vc_ref, bias_ref, wup_ref, wdown_ref, o_ref, wup_bf_ref, wdown_bf_ref):
    wup_bf_ref[...] = wup_ref[0].astype(BF16)
    wdown_bf_ref[...] = wdown_ref[0].astype(BF16)
    rows = q_ref.shape[1] // GRID_W
    band = NA_KH * GRID_W
    kc = kc_ref[0]
    vc = vc_ref[0]

    def group(g, carry):
        rs = [g * NA_ROW_GROUP + j for j in range(NA_ROW_GROUP)]
        starts = [jnp.clip(r - NA_KH // 2, 0, rows - NA_KH) for r in rs]
        q0 = [pl.multiple_of(r * GRID_W, GRID_W) for r in rs]
        k0 = [pl.multiple_of(s * GRID_W, GRID_W) for s in starts]
        lhs = [_stack_heads(q_ref[0, pl.ds(a, GRID_W), :]) for a in q0]
        variant = [s - r + (NA_KH - 1) for s, r in zip(starts, rs)]
        s_win = [_scores(l, k_ref[0, pl.ds(a, band), :]) for l, a in zip(lhs, k0)]
        s_ctx = [_scores(l, kc) for l in lhs]
        n_slabs = 2 * GRID_W // NA_SLAB
        for j in range(NA_ROW_GROUP):
            pw, pc, inv = [], [], []
            for t in range(n_slabs):
                sl = slice(t * NA_SLAB, (t + 1) * NA_SLAB)
                sw = s_win[j][sl] + bias_ref[0, variant[j], 0, sl, :]
                sc = s_ctx[j][sl]
                m = jnp.maximum(jnp.max(sw, axis=-1, keepdims=True), jnp.max(sc, axis=-1, keepdims=True))
                ew = jnp.exp(sw - m)
                ec = jnp.exp(sc - m)
                inv.append(1.0 / (jnp.sum(ew, axis=-1, keepdims=True) + jnp.sum(ec, axis=-1, keepdims=True)))
                pw.append(ew.astype(BF16))
                pc.append(ec.astype(BF16))
            o = jnp.dot(jnp.concatenate(pw, axis=0), v_ref[0, pl.ds(k0[j], band), :], preferred_element_type=F32)
            o = o + jnp.dot(jnp.concatenate(pc, axis=0), vc, preferred_element_type=F32)
            o = o * jnp.concatenate(inv, axis=0)
            o_ref[0, pl.ds(q0[j], GRID_W), :] = _unstack_heads(o, GRID_W).astype(BF16)
        return carry

    lax.fori_loop(0, rows // NA_ROW_GROUP, group, 0)


def _neighbourhood_attention(q, k, v, kc, vc, bias, w_up, w_down, layer):
    b, t, _ = q.shape
    ctx = kc.shape[1]
    steps = b * N_HEAD_BLOCKS
    assert (t // GRID_W) % NA_ROW_GROUP == 0
    seq = lambda length: pl.BlockSpec((1, length, LANES), lambda bi, hb: (bi, 0, hb))

    def slab_specs(w):
        rows, cols = w.shape[1:]
        assert rows % (steps * BF16_ROWS) == 0
        slab = rows // steps
        return (pl.BlockSpec((1, slab, cols), lambda bi, hb: (layer, bi * N_HEAD_BLOCKS + hb, 0)),
                pl.BlockSpec((slab, cols), lambda bi, hb: (bi * N_HEAD_BLOCKS + hb, 0)),
                jax.ShapeDtypeStruct((rows, cols), BF16))

    up_in, up_out, up_shape = slab_specs(w_up)
    down_in, down_out, down_shape = slab_specs(w_down)
    return pl.pallas_call(
        _na_body,
        grid=(b, N_HEAD_BLOCKS),
        in_specs=[seq(t), seq(t), seq(t), seq(ctx), seq(ctx),
                  pl.BlockSpec((1, NA_KH, 1) + bias.shape[3:], lambda bi, hb: (layer, 0, hb, 0, 0)),
                  up_in, down_in],
        out_specs=[seq(t), up_out, down_out],
        out_shape=[jax.ShapeDtypeStruct((b, t, NA_DIM), BF16), up_shape, down_shape],
        compiler_params=_params(("arbitrary", "arbitrary")),
        name="na_attention",
    )(q, k, v, kc, vc, bias, w_up, w_down)


def _ctx_attn_body(q_ref, k_ref, v_ref, o_ref):
    n = q_ref.shape[1]
    lhs = _stack_heads(q_ref[0])
    s = _scores(lhs, k_ref[0])
    p = jnp.exp(s - jnp.max(s, axis=-1, keepdims=True))
    denom = jnp.sum(p, axis=-1, keepdims=True)
    o = jnp.dot(p.astype(BF16), v_ref[0], preferred_element_type=F32)
    o_ref[0] = _unstack_heads(o / denom, n).astype(BF16)


def _context_attention(q, k, v):
    b, n, _ = q.shape
    spec = pl.BlockSpec((1, n, LANES), lambda bi, hb: (bi, 0, hb))
    return pl.pallas_call(
        _ctx_attn_body,
        grid=(b, N_HEAD_BLOCKS),
        in_specs=[spec, spec, spec],
        out_specs=spec,
        out_shape=jax.ShapeDtypeStruct((b, n, NA_DIM), BF16),
        compiler_params=_params(("arbitrary", "arbitrary")),
        name="ctx_attention",
    )(q, k, v)


def _mix_body(x_ref, up_ref, uc_ref, un_ref, a_ref, mod_ref, win_ref, g_ref, dw_ref, dwb_ref, lng_ref, lnb_ref,
              wconv_ref, wna_ref, wout_ref, o_ref):
    i = pl.program_id(1)
    tm = x_ref.shape[1]
    x = x_ref[0]
    mod = mod_ref[0, 0]
    h = _rms_modulate(x, g_ref[0], mod[0:1], mod[1:2]).astype(BF16)
    gates = jax.nn.sigmoid(jnp.dot(h, win_ref[:, QKV_END:], preferred_element_type=F32))
    y_attn = jnp.dot(a_ref[0], wna_ref[0], preferred_element_type=F32)
    prev = up_ref[0].astype(F32)
    nxt = un_ref[0].astype(F32)
    prev = jnp.where(i > 0, prev, jnp.zeros_like(prev))
    nxt = jnp.where(i < pl.num_programs(1) - 1, nxt, jnp.zeros_like(nxt))
    ucat = jnp.concatenate([prev, uc_ref[0].astype(F32), nxt], axis=0)
    n = tm + 2 * CONV_HALO
    assert CONV_HALO - CONV_K // 2 == 1 and CONV_K < 4 * SUBLANES
    acc = jnp.zeros((tm, CONV_DIM), F32) + dwb_ref[0]
    for b in range(SUBLANES):
        shifted = ucat if b == 0 else pltpu.roll(ucat, n - b, 0)
        for a in range(4):
            j = SUBLANES * a + b
            if 1 <= j <= CONV_K:
                acc = acc + dw_ref[0, j - 1:j, :] * shifted[SUBLANES * a:SUBLANES * a + tm]
    mu = jnp.mean(acc, axis=-1, keepdims=True)
    cen = acc - mu
    var = jnp.mean(cen * cen, axis=-1, keepdims=True)
    feat = jax.nn.silu(cen * lax.rsqrt(var + EPS) * lng_ref[0] + lnb_ref[0]).astype(BF16)
    y_conv = jnp.dot(feat, wconv_ref[0], preferred_element_type=F32)
    merged = (gates[:, :D_MODEL] * y_conv + gates[:, D_MODEL:] * y_attn).astype(BF16)
    o_ref[0] = x + mod[2:3] * jnp.dot(merged, wout_ref[0], preferred_element_type=F32)


def _mix(x, u, attn, layer, mods, mod_row, w_in, p, tm):
    b, t, d = x.shape
    per = tm // CONV_HALO
    last = t // CONV_HALO - 1
    seq = lambda width: pl.BlockSpec((1, tm, width), lambda bi, i: (bi, i, 0))
    weights = (p["norm1_g"], p["conv_dw"], p["conv_dw_b"], p["conv_ln_g"], p["conv_ln_b"],
               p["w_conv_out"], p["w_na_out"], p["w_out"])
    return pl.pallas_call(
        _mix_body,
        grid=(b, t // tm),
        in_specs=[
            seq(d),
            pl.BlockSpec((1, CONV_HALO, CONV_DIM), lambda bi, i: (bi, jnp.maximum(i * per - 1, 0), 0)),
            seq(CONV_DIM),
            pl.BlockSpec((1, CONV_HALO, CONV_DIM), lambda bi, i: (bi, jnp.minimum((i + 1) * per, last), 0)),
            seq(NA_DIM),
            _mod_spec(layer, mod_row),
            _whole_spec(w_in),
        ] + [_layer_spec(w, layer) for w in weights],
        out_specs=seq(d),
        out_shape=jax.ShapeDtypeStruct((b, t, d), F32),
        compiler_params=_params(("arbitrary", "arbitrary")),
        name="mix",
    )(x, u, u, u, attn, mods, w_in, *weights)


def _ffn_body(final_norm, x_ref, xp_ref, xn_ref, mod_ref, wup_ref, wdown_ref, g_ref, dw_ref, dwb_ref, fg_ref,
              o_ref):
    i = pl.program_id(1)
    tm = x_ref.shape[1]
    mod = mod_ref[0, 0]
    gain = g_ref[0]
    x = x_ref[0]
    hp = _rms_modulate(xp_ref[0], gain, mod[3:4], mod[4:5])
    hn = _rms_modulate(xn_ref[0], gain, mod[3:4], mod[4:5])
    hp = jnp.where(i > 0, hp, jnp.zeros_like(hp))
    hn = jnp.where(i < pl.num_programs(1) - 1, hn, jnp.zeros_like(hn))
    h = jnp.concatenate([hp, _rms_modulate(x, gain, mod[3:4], mod[4:5]), hn], axis=0).astype(BF16)
    n = tm + 2 * FFN_HALO
    assert FFN_CONV_K == 3

    def conv(cols, up):
        before = pltpu.roll(up, 1, 0)[FFN_HALO:FFN_HALO + tm]
        after = pltpu.roll(up, n - 1, 0)[FFN_HALO:FFN_HALO + tm]
        return (dwb_ref[0, :, cols] + dw_ref[0, 0:1, cols] * before
                + dw_ref[0, 1:2, cols] * up[FFN_HALO:FFN_HALO + tm] + dw_ref[0, 2:3, cols] * after)

    value_cols = slice(0, FFN_DIM)
    gate_cols = slice(FFN_DIM, 2 * FFN_DIM)
    up_gate = jnp.dot(h, wup_ref[:, gate_cols], preferred_element_type=F32)
    up_value = jnp.dot(h, wup_ref[:, value_cols], preferred_element_type=F32)
    act = (jax.nn.silu(conv(gate_cols, up_gate)) * conv(value_cols, up_value)).astype(BF16)
    y = x + mod[5:6] * jnp.dot(act, wdown_ref[...], preferred_element_type=F32)
    if final_norm:
        y = y * lax.rsqrt(jnp.mean(y * y, axis=-1, keepdims=True) + EPS) * fg_ref[...]
    o_ref[0] = y


def _ffn(x, layer, mods, mod_row, w_up, w_down, p, final_gain, final_norm, tm):
    b, t, d = x.shape
    per = tm // FFN_HALO
    last = t // FFN_HALO - 1
    seq = pl.BlockSpec((1, tm, d), lambda bi, i: (bi, i, 0))
    weights = (p["norm2_g"], p["ffn_dw"], p["ffn_dw_b"])
    return pl.pallas_call(
        functools.partial(_ffn_body, final_norm),
        grid=(b, t // tm),
        in_specs=[
            seq,
            pl.BlockSpec((1, FFN_HALO, d), lambda bi, i: (bi, jnp.maximum(i * per - 1, 0), 0)),
            pl.BlockSpec((1, FFN_HALO, d), lambda bi, i: (bi, jnp.minimum((i + 1) * per, last), 0)),
            _mod_spec(layer, mod_row),
            _whole_spec(w_up),
            _whole_spec(w_down),
        ] + [_layer_spec(w, layer) for w in weights] + [_whole_spec(final_gain)],
        out_specs=seq,
        out_shape=jax.ShapeDtypeStruct((b, t, d), F32),
        compiler_params=_params(("arbitrary", "arbitrary")),
        name="ffn",
    )(x, x, x, mods, w_up, w_down, *weights, final_gain)


def _rope_tables(seq_len):
    half = HEAD_DIM // 2
    inv_freq = ROPE_THETA ** (-jnp.arange(0, half, 2, dtype=F32) / half)
    t = jnp.arange(seq_len)
    ang_r = (t // GRID_W).astype(F32)[:, None] * inv_freq[None, :]
    ang_c = (t % GRID_W).astype(F32)[:, None] * inv_freq[None, :]
    cos = jnp.concatenate([jnp.cos(ang_r)] * 2 + [jnp.cos(ang_c)] * 2, axis=-1)
    sin = jnp.concatenate([-jnp.sin(ang_r), jnp.sin(ang_r), -jnp.sin(ang_c), jnp.sin(ang_c)], axis=-1)
    return jnp.tile(cos, (1, HEADS_PER_BLOCK)), jnp.tile(sin, (1, HEADS_PER_BLOCK))


def kernel(x, c, ctx, c_ctx, w_ada, b_ada, norm1_g, w_in, conv_dw, conv_dw_b, conv_ln_g, conv_ln_b, w_conv_out,
           na_rpb, w_na_out, w_out, norm2_g, w_up, ffn_dw, ffn_dw_b, w_down, final_norm_g):
    batch, seq_len, d = x.shape
    depth = w_ada.shape[0]
    ctx_len = ctx.shape[1]
    assert d == D_MODEL and seq_len % GRID_W == 0 and batch + 1 <= MOD_ROWS

    cvec = jnp.zeros((MOD_ROWS, d), F32).at[:batch].set(c).at[batch].set(c_ctx)
    mods, bias, (w_conv_bf, w_na_bf, w_out_bf) = _ada_and_window_bias(
        cvec, w_ada, b_ada, na_rpb, (w_conv_out, w_na_out, w_out))
    mods = mods.reshape(depth, MOD_ROWS, N_MOD, d)
    lat_row = lambda bi: bi
    ctx_row = lambda bi: batch

    cos, sin = _rope_tables(seq_len)
    cos_id = jnp.ones((ctx_len, LANES), F32)
    sin_id = jnp.zeros((ctx_len, LANES), F32)
    rows = lambda v: v.reshape(depth, 1, -1)
    p = {
        "norm1_g": rows(norm1_g), "norm2_g": rows(norm2_g),
        "conv_dw": conv_dw, "conv_dw_b": rows(conv_dw_b), "conv_ln_g": rows(conv_ln_g), "conv_ln_b": rows(conv_ln_b),
        "ffn_dw": ffn_dw, "ffn_dw_b": rows(ffn_dw_b),
        "w_conv_out": w_conv_bf, "w_na_out": w_na_bf, "w_out": w_out_bf,
    }
    fin = final_norm_g.reshape(1, d)

    xc = ctx
    for l in range(depth):
        last = l == depth - 1
        u_c, q_c, k_c, v_c, w_in_l = _inproj(xc, l, mods, ctx_row, p["norm1_g"], cos_id, sin_id, w_in, ctx_len)
        u, q, k, v = _inproj(x, l, mods, lat_row, p["norm1_g"], cos, sin, w_in_l, INPROJ_TM)
        attn, w_up_l, w_down_l = _neighbourhood_attention(q, k, v, k_c, v_c, bias, w_up, w_down, l)
        x = _mix(x, u, attn, l, mods, lat_row, w_in_l, p, MIX_TM)
        x = _ffn(x, l, mods, lat_row, w_up_l, w_down_l, p, fin, last, FFN_TM)
        if not last:
            attn_c = _context_attention(q_c, k_c, v_c)
            xc = _mix(xc, u_c, attn_c, l, mods, ctx_row, w_in_l, p, ctx_len)
            xc = _ffn(xc, l, mods, ctx_row, w_up_l, w_down_l, p, fin, False, ctx_len)
    return x
```

```python
import functools

import jax
import jax.numpy as jnp
from jax import lax
from jax.experimental import pallas as pl
from jax.experimental.pallas import tpu as pltpu

D_MODEL = 1024
GRID_W = 64
CONV_DIM = 512
CONV_K = 31
NA_HEADS = 8
HEAD_DIM = 64
NA_DIM = NA_HEADS * HEAD_DIM
NA_KH = 8
NA_KW = 16
ROPE_THETA = 10000.0
FFN_DIM = 2816
FFN_CONV_K = 3
EPS = 1e-6
GLU_END = 2 * CONV_DIM
QKV_END = GLU_END + 3 * NA_DIM

LANES = 128
SUBLANES = 8
BF16_ROWS = 16
V7X_VMEM_BYTES = 64 * 1024 * 1024
VMEM_LIMIT = V7X_VMEM_BYTES * 7 // 8

HEADS_PER_BLOCK = LANES // HEAD_DIM
N_HEAD_BLOCKS = NA_HEADS // HEADS_PER_BLOCK
CONV_HALO = BF16_ROWS
FFN_HALO = SUBLANES
MOD_ROWS = 8
N_MOD = 6
NA_ROW_GROUP = 8
NA_SLAB = 32

INPROJ_TM = 1024
MIX_TM = 1024
FFN_TM = 1024
ADA_TK = 256

BF16 = jnp.bfloat16
F32 = jnp.float32


def _params(semantics):
    return pltpu.CompilerParams(dimension_semantics=semantics, vmem_limit_bytes=VMEM_LIMIT)


def _layer_spec(arr, layer):
    tail = (0,) * (arr.ndim - 1)
    return pl.BlockSpec((1,) + arr.shape[1:], lambda *_: (layer,) + tail, pipeline_mode=pl.Buffered(1))


def _whole_spec(arr):
    zeros = (0,) * arr.ndim
    return pl.BlockSpec(arr.shape, lambda *_: zeros, pipeline_mode=pl.Buffered(1))


def _mod_spec(layer, mod_row):
    return pl.BlockSpec((1, 1, N_MOD, D_MODEL), lambda bi, i: (layer, mod_row(bi), 0, 0))


def _rms_modulate(x, gain, shift, scale):
    return x * lax.rsqrt(jnp.mean(x * x, axis=-1, keepdims=True) + EPS) * (gain * (1.0 + scale)) + shift


def _ada_body(c_ref, w_ref, b_ref, o_ref):
    s = jax.nn.silu(c_ref[...]).astype(BF16)
    part = jnp.dot(s, w_ref[0].astype(BF16), preferred_element_type=F32)

    @pl.when(pl.program_id(1) == 0)
    def _():
        o_ref[0] = part + b_ref[0]

    @pl.when(pl.program_id(1) > 0)
    def _():
        o_ref[0] += part


def _ada_bias_body(n_cast, c_ref, w_ref, b_ref, rpb_ref, *refs):
    f32_refs, (o_ref, bias_ref), bf16_refs = refs[:n_cast], refs[n_cast:n_cast + 2], refs[n_cast + 2:]
    _ada_body(c_ref, w_ref, b_ref, o_ref)
    _bias_body(rpb_ref, bias_ref)
    for src, dst in zip(f32_refs, bf16_refs):
        dst[...] = src[...].astype(BF16)


def _ada_and_window_bias(cvec, w_ada, b_ada, rpb, cast_weights):
    depth, d, n = w_ada.shape
    steps = d // ADA_TK
    assert steps == N_HEAD_BLOCKS
    tile = (HEADS_PER_BLOCK * GRID_W, NA_KH * GRID_W)

    def slab(w):
        rows, cols = w.shape[1:]
        assert rows % (steps * BF16_ROWS) == 0
        return pl.BlockSpec((1, rows // steps, cols), lambda l, k: (l, k, 0))

    outs = pl.pallas_call(
        functools.partial(_ada_bias_body, len(cast_weights)),
        grid=(depth, steps),
        in_specs=[
            pl.BlockSpec((MOD_ROWS, ADA_TK), lambda l, k: (0, k)),
            pl.BlockSpec((1, ADA_TK, n), lambda l, k: (l, k, 0)),
            pl.BlockSpec((1, 1, n), lambda l, k: (l, 0, 0)),
            pl.BlockSpec(memory_space=pltpu.SMEM),
        ] + [slab(w) for w in cast_weights],
        out_specs=[pl.BlockSpec((1, MOD_ROWS, n), lambda l, k: (l, 0, 0)),
                   pl.BlockSpec((1, NA_KH, 1) + tile, lambda l, hb: (l, 0, hb, 0, 0))]
        + [slab(w) for w in cast_weights],
        out_shape=[jax.ShapeDtypeStruct((depth, MOD_ROWS, n), F32),
                   jax.ShapeDtypeStruct((depth, NA_KH, N_HEAD_BLOCKS) + tile, F32)]
        + [jax.ShapeDtypeStruct(w.shape, BF16) for w in cast_weights],
        compiler_params=_params(("arbitrary", "arbitrary")),
        name="ada_bias",
    )(cvec, w_ada, b_ada.reshape(depth, 1, n), rpb.reshape(-1), *cast_weights)
    return outs[0], outs[1], outs[2:]


def _bias_body(rpb_ref, o_ref):
    layer = pl.program_id(0)
    hb = pl.program_id(1)
    c = lax.broadcasted_iota(jnp.int32, (GRID_W, GRID_W), 0)
    cp = lax.broadcasted_iota(jnp.int32, (GRID_W, GRID_W), 1)
    start = jnp.clip(c - NA_KW // 2, 0, GRID_W - NA_KW)
    inside = (cp >= start) & (cp < start + NA_KW)
    diff = cp - c + (NA_KW - 1)
    n_dr = 2 * NA_KH - 1
    n_dc = 2 * NA_KW - 1
    for hh in range(HEADS_PER_BLOCK):
        h = (layer * N_HEAD_BLOCKS + hb) * HEADS_PER_BLOCK + hh
        for i in range(n_dr):
            val = jnp.zeros((GRID_W, GRID_W), F32)
            for d in range(n_dc):
                val = jnp.where(diff == d, rpb_ref[(h * n_dr + i) * n_dc + d], val)
            val = jnp.where(inside, val, -jnp.inf)
            for v in range(NA_KH):
                j = i - v
                if 0 <= j < NA_KH:
                    o_ref[0, v, 0, hh * GRID_W:(hh + 1) * GRID_W, j * GRID_W:(j + 1) * GRID_W] = val


def _rope(t, cos, sin, swap_up):
    partner = jnp.where(swap_up, pltpu.roll(t, LANES - HEAD_DIM // 4, 1), pltpu.roll(t, HEAD_DIM // 4, 1))
    return t * cos + partner * sin


def _inproj_body(cast_weights, x_ref, mod_ref, g_ref, cos_ref, sin_ref, win_ref, u_ref, q_ref, k_ref, v_ref,
                 *wb_refs):
    if cast_weights:
        (w_ref,) = wb_refs

        @pl.when((pl.program_id(0) == 0) & (pl.program_id(1) == 0))
        def _():
            w_ref[...] = win_ref[0].astype(BF16)
    else:
        w_ref = win_ref
    mod = mod_ref[0, 0]
    h = _rms_modulate(x_ref[0], g_ref[0], mod[0:1], mod[1:2]).astype(BF16)
    pg = jnp.dot(h, w_ref[:, :GLU_END], preferred_element_type=F32)
    u_ref[0] = (pg[:, :CONV_DIM] * jax.nn.sigmoid(pg[:, CONV_DIM:])).astype(BF16)
    pq = jnp.dot(h, w_ref[:, GLU_END:QKV_END], preferred_element_type=F32)
    cos = cos_ref[...]
    sin = sin_ref[...]
    lane = lax.broadcasted_iota(jnp.int32, cos.shape, 1)
    swap_up = (lane % (HEAD_DIM // 2)) < (HEAD_DIM // 4)
    for blk in range(N_HEAD_BLOCKS):
        sl = slice(blk * LANES, (blk + 1) * LANES)
        q_ref[0, :, sl] = (_rope(pq[:, sl], cos, sin, swap_up) * (HEAD_DIM ** -0.5)).astype(BF16)
        ks = slice(NA_DIM + blk * LANES, NA_DIM + (blk + 1) * LANES)
        k_ref[0, :, sl] = _rope(pq[:, ks], cos, sin, swap_up).astype(BF16)
    v_ref[0] = pq[:, 2 * NA_DIM:].astype(BF16)


def _inproj(x, layer, mods, mod_row, gain, cos, sin, w_in, tm):
    b, t, d = x.shape
    cast_weights = w_in.ndim == 3
    seq_spec = lambda width: pl.BlockSpec((1, tm, width), lambda bi, i: (bi, i, 0))
    out = jax.ShapeDtypeStruct((b, t, NA_DIM), BF16)
    out_specs = [seq_spec(CONV_DIM), seq_spec(NA_DIM), seq_spec(NA_DIM), seq_spec(NA_DIM)]
    out_shape = [jax.ShapeDtypeStruct((b, t, CONV_DIM), BF16), out, out, out]
    if cast_weights:
        out_specs.append(pl.BlockSpec(w_in.shape[1:], lambda bi, i: (0, 0)))
        out_shape.append(jax.ShapeDtypeStruct(w_in.shape[1:], BF16))
    return pl.pallas_call(
        functools.partial(_inproj_body, cast_weights),
        grid=(b, t // tm),
        in_specs=[
            seq_spec(d),
            _mod_spec(layer, mod_row),
            _layer_spec(gain, layer),
            pl.BlockSpec((tm, LANES), lambda bi, i: (i, 0)),
            pl.BlockSpec((tm, LANES), lambda bi, i: (i, 0)),
            _layer_spec(w_in, layer) if cast_weights else _whole_spec(w_in),
        ],
        out_specs=out_specs,
        out_shape=out_shape,
        compiler_params=_params(("arbitrary", "arbitrary")),
        name="inproj",
    )(x, mods, gain, cos, sin, w_in)


def _stack_heads(t):
    lane = lax.broadcasted_iota(jnp.int32, t.shape, 1)
    zero = jnp.zeros_like(t)
    return jnp.concatenate([jnp.where(lane < HEAD_DIM, t, zero), jnp.where(lane >= HEAD_DIM, t, zero)], axis=0)


def _unstack_heads(o, rows):
    lane = lax.broadcasted_iota(jnp.int32, (rows, LANES), 1)
    return jnp.where(lane < HEAD_DIM, o[:rows], o[rows:])


def _scores(lhs, keys):
    return lax.dot_general(lhs, keys, (((1,), (1,)), ((), ())), preferred_element_type=F32)


def _na_body(q_ref, k_ref, v_ref, kc_ref, vc_ref, bias_ref, wup_ref, wdown_ref, o_ref, wup_bf_ref, wdown_bf_ref):
    wup_bf_ref[...] = wup_ref[0].astype(BF16)
    wdown_bf_ref[...] = wdown_ref[0].astype(BF16)
    rows = q_ref.shape[1] // GRID_W
    band = NA_KH * GRID_W
    kc = kc_ref[0]
    vc = vc_ref[0]

    def group(g, carry):
        rs = [g * NA_ROW_GROUP + j for j in range(NA_ROW_GROUP)]
        starts = [jnp.clip(r - NA_KH // 2, 0, rows - NA_KH) for r in rs]
        q0 = [pl.multiple_of(r * GRID_W, GRID_W) for r in rs]
        k0 = [pl.multiple_of(s * GRID_W, GRID_W) for s in starts]
        lhs = [_stack_heads(q_ref[0, pl.ds(a, GRID_W), :]) for a in q0]
        variant = [s - r + (NA_KH - 1) for s, r in zip(starts, rs)]
        s_win = [_scores(l, k_ref[0, pl.ds(a, band), :]) for l, a in zip(lhs, k0)]
        s_ctx = [_scores(l, kc) for l in lhs]
        n_slabs = 2 * GRID_W // NA_SLAB
        for j in range(NA_ROW_GROUP):
            pw, pc, inv = [], [], []
            for t in range(n_slabs):
                sl = slice(t * NA_SLAB, (t + 1) * NA_SLAB)
                sw = s_win[j][sl] + bias_ref[0, variant[j], 0, sl, :]
                sc = s_ctx[j][sl]
                m = jnp.maximum(jnp.max(sw, axis=-1, keepdims=True), jnp.max(sc, axis=-1, keepdims=True))
                ew = jnp.exp(sw - m)
                ec = jnp.exp(sc - m)
                inv.append(1.0 / (jnp.sum(ew, axis=-1, keepdims=True) + jnp.sum(ec, axis=-1, keepdims=True)))
                pw.append(ew.astype(BF16))
                pc.append(ec.astype(BF16))
            o = jnp.dot(jnp.concatenate(pw, axis=0), v_ref[0, pl.ds(k0[j], band), :], preferred_element_type=F32)
            o = o + jnp.dot(jnp.concatenate(pc, axis=0), vc, preferred_element_type=F32)
            o = o * jnp.concatenate(inv, axis=0)
            o_ref[0, pl.ds(q0[j], GRID_W), :] = _unstack_heads(o, GRID_W).astype(BF16)
        return carry

    lax.fori_loop(0, rows // NA_ROW_GROUP, group, 0)


def _neighbourhood_attention(q, k, v, kc, vc, bias, w_up, w_down, layer):
    b, t, _ = q.shape
    ctx = kc.shape[1]
    steps = b * N_HEAD_BLOCKS
    assert (t // GRID_W) % NA_ROW_GROUP == 0
    seq = lambda length: pl.BlockSpec((1, length, LANES), lambda bi, hb: (bi, 0, hb))

    def slab_specs(w):
        rows, cols = w.shape[1:]
        assert rows % (steps * BF16_ROWS) == 0
        slab = rows // steps
        return (pl.BlockSpec((1, slab, cols), lambda bi, hb: (layer, bi * N_HEAD_BLOCKS + hb, 0)),
                pl.BlockSpec((slab, cols), lambda bi, hb: (bi * N_HEAD_BLOCKS + hb, 0)),
                jax.ShapeDtypeStruct((rows, cols), BF16))

    up_in, up_out, up_shape = slab_specs(w_up)
    down_in, down_out, down_shape = slab_specs(w_down)
    return pl.pallas_call(
        _na_body,
        grid=(b, N_HEAD_BLOCKS),
        in_specs=[seq(t), seq(t), seq(t), seq(ctx), seq(ctx),
                  pl.BlockSpec((1, NA_KH, 1) + bias.shape[3:], lambda bi, hb: (layer, 0, hb, 0, 0)),
                  up_in, down_in],
        out_specs=[seq(t), up_out, down_out],
        out_shape=[jax.ShapeDtypeStruct((b, t, NA_DIM), BF16), up_shape, down_shape],
        compiler_params=_params(("arbitrary", "arbitrary")),
        name="na_attention",
    )(q, k, v, kc, vc, bias, w_up, w_down)


def _ctx_attn_body(q_ref, k_ref, v_ref, o_ref):
    n = q_ref.shape[1]
    lhs = _stack_heads(q_ref[0])
    s = _scores(lhs, k_ref[0])
    p = jnp.exp(s - jnp.max(s, axis=-1, keepdims=True))
    denom = jnp.sum(p, axis=-1, keepdims=True)
    o = jnp.dot(p.astype(BF16), v_ref[0], preferred_element_type=F32)
    o_ref[0] = _unstack_heads(o / denom, n).astype(BF16)


def _context_attention(q, k, v):
    b, n, _ = q.shape
    spec = pl.BlockSpec((1, n, LANES), lambda bi, hb: (bi, 0, hb))
    return pl.pallas_call(
        _ctx_attn_body,
        grid=(b, N_HEAD_BLOCKS),
        in_specs=[spec, spec, spec],
        out_specs=spec,
        out_shape=jax.ShapeDtypeStruct((b, n, NA_DIM), BF16),
        compiler_params=_params(("arbitrary", "arbitrary")),
        name="ctx_attention",
    )(q, k, v)


def _mix_body(x_ref, up_ref, uc_ref, un_ref, a_ref, mod_ref, win_ref, g_ref, dw_ref, dwb_ref, lng_ref, lnb_ref,
              wconv_ref, wna_ref, wout_ref, o_ref):
    i = pl.program_id(1)
    tm = x_ref.shape[1]
    x = x_ref[0]
    mod = mod_ref[0, 0]
    h = _rms_modulate(x, g_ref[0], mod[0:1], mod[1:2]).astype(BF16)
    gates = jax.nn.sigmoid(jnp.dot(h, win_ref[:, QKV_END:], preferred_element_type=F32))
    y_attn = jnp.dot(a_ref[0], wna_ref[0], preferred_element_type=F32)
    prev = up_ref[0].astype(F32)
    nxt = un_ref[0].astype(F32)
    prev = jnp.where(i > 0, prev, jnp.zeros_like(prev))
    nxt = jnp.where(i < pl.num_programs(1) - 1, nxt, jnp.zeros_like(nxt))
    ucat = jnp.concatenate([prev, uc_ref[0].astype(F32), nxt], axis=0)
    n = tm + 2 * CONV_HALO
    assert CONV_HALO - CONV_K // 2 == 1 and CONV_K < 4 * SUBLANES
    acc = jnp.zeros((tm, CONV_DIM), F32) + dwb_ref[0]
    for b in range(SUBLANES):
        shifted = ucat if b == 0 else pltpu.roll(ucat, n - b, 0)
        for a in range(4):
            j = SUBLANES * a + b
            if 1 <= j <= CONV_K:
                acc = acc + dw_ref[0, j - 1:j, :] * shifted[SUBLANES * a:SUBLANES * a + tm]
    mu = jnp.mean(acc, axis=-1, keepdims=True)
    cen = acc - mu
    var = jnp.mean(cen * cen, axis=-1, keepdims=True)
    feat = jax.nn.silu(cen * lax.rsqrt(var + EPS) * lng_ref[0] + lnb_ref[0]).astype(BF16)
    y_conv = jnp.dot(feat, wconv_ref[0], preferred_element_type=F32)
    merged = (gates[:, :D_MODEL] * y_conv + gates[:, D_MODEL:] * y_attn).astype(BF16)
    o_ref[0] = x + mod[2:3] * jnp.dot(merged, wout_ref[0], preferred_element_type=F32)


def _mix(x, u, attn, layer, mods, mod_row, w_in, p, tm):
    b, t, d = x.shape
    per = tm // CONV_HALO
    last = t // CONV_HALO - 1
    seq = lambda width: pl.BlockSpec((1, tm, width), lambda bi, i: (bi, i, 0))
    weights = (p["norm1_g"], p["conv_dw"], p["conv_dw_b"], p["conv_ln_g"], p["conv_ln_b"],
               p["w_conv_out"], p["w_na_out"], p["w_out"])
    return pl.pallas_call(
        _mix_body,
        grid=(b, t // tm),
        in_specs=[
            seq(d),
            pl.BlockSpec((1, CONV_HALO, CONV_DIM), lambda bi, i: (bi, jnp.maximum(i * per - 1, 0), 0)),
            seq(CONV_DIM),
            pl.BlockSpec((1, CONV_HALO, CONV_DIM), lambda bi, i: (bi, jnp.minimum((i + 1) * per, last), 0)),
            seq(NA_DIM),
            _mod_spec(layer, mod_row),
            _whole_spec(w_in),
        ] + [_layer_spec(w, layer) for w in weights],
        out_specs=seq(d),
        out_shape=jax.ShapeDtypeStruct((b, t, d), F32),
        compiler_params=_params(("arbitrary", "arbitrary")),
        name="mix",
    )(x, u, u, u, attn, mods, w_in, *weights)


def _ffn_body(final_norm, x_ref, xp_ref, xn_ref, mod_ref, wup_ref, wdown_ref, g_ref, dw_ref, dwb_ref, fg_ref,
              o_ref):
    i = pl.program_id(1)
    tm = x_ref.shape[1]
    mod = mod_ref[0, 0]
    gain = g_ref[0]
    x = x_ref[0]
    hp = _rms_modulate(xp_ref[0], gain, mod[3:4], mod[4:5])
    hn = _rms_modulate(xn_ref[0], gain, mod[3:4], mod[4:5])
    hp = jnp.where(i > 0, hp, jnp.zeros_like(hp))
    hn = jnp.where(i < pl.num_programs(1) - 1, hn, jnp.zeros_like(hn))
    h = jnp.concatenate([hp, _rms_modulate(x, gain, mod[3:4], mod[4:5]), hn], axis=0).astype(BF16)
    n = tm + 2 * FFN_HALO
    assert FFN_CONV_K == 3

    def conv(cols, up):
        before = pltpu.roll(up, 1, 0)[FFN_HALO:FFN_HALO + tm]
        after = pltpu.roll(up, n - 1, 0)[FFN_HALO:FFN_HALO + tm]
        return (dwb_ref[0, :, cols] + dw_ref[0, 0:1, cols] * before
                + dw_ref[0, 1:2, cols] * up[FFN_HALO:FFN_HALO + tm] + dw_ref[0, 2:3, cols] * after)

    value_cols = slice(0, FFN_DIM)
    gate_cols = slice(FFN_DIM, 2 * FFN_DIM)
    up_gate = jnp.dot(h, wup_ref[:, gate_cols], preferred_element_type=F32)
    up_value = jnp.dot(h, wup_ref[:, value_cols], preferred_element_type=F32)
    act = (jax.nn.silu(conv(gate_cols, up_gate)) * conv(value_cols, up_value)).astype(BF16)
    y = x + mod[5:6] * jnp.dot(act, wdown_ref[...], preferred_element_type=F32)
    if final_norm:
        y = y * lax.rsqrt(jnp.mean(y * y, axis=-1, keepdims=True) + EPS) * fg_ref[...]
    o_ref[0] = y


def _ffn(x, layer, mods, mod_row, w_up, w_down, p, final_gain, final_norm, tm):
    b, t, d = x.shape
    per = tm // FFN_HALO
    last = t // FFN_HALO - 1
    seq = pl.BlockSpec((1, tm, d), lambda bi, i: (bi, i, 0))
    weights = (p["norm2_g"], p["ffn_dw"], p["ffn_dw_b"])
    return pl.pallas_call(
        functools.partial(_ffn_body, final_norm),
        grid=(b, t // tm),
        in_specs=[
            seq,
            pl.BlockSpec((1, FFN_HALO, d), lambda bi, i: (bi, jnp.maximum(i * per - 1, 0), 0)),
            pl.BlockSpec((1, FFN_HALO, d), lambda bi, i: (bi, jnp.minimum((i + 1) * per, last), 0)),
            _mod_spec(layer, mod_row),
            _whole_spec(w_up),
            _whole_spec(w_down),
        ] + [_layer_spec(w, layer) for w in weights] + [_whole_spec(final_gain)],
        out_specs=seq,
        out_shape=jax.ShapeDtypeStruct((b, t, d), F32),
        compiler_params=_params(("arbitrary", "arbitrary")),
        name="ffn",
    )(x, x, x, mods, w_up, w_down, *weights, final_gain)


def _rope_tables(seq_len):
    half = HEAD_DIM // 2
    inv_freq = ROPE_THETA ** (-jnp.arange(0, half, 2, dtype=F32) / half)
    t = jnp.arange(seq_len)
    ang_r = (t // GRID_W).astype(F32)[:, None] * inv_freq[None, :]
    ang_c = (t % GRID_W).astype(F32)[:, None] * inv_freq[None, :]
    cos = jnp.concatenate([jnp.cos(ang_r)] * 2 + [jnp.cos(ang_c)] * 2, axis=-1)
    sin = jnp.concatenate([-jnp.sin(ang_r), jnp.sin(ang_r), -jnp.sin(ang_c), jnp.sin(ang_c)], axis=-1)
    return jnp.tile(cos, (1, HEADS_PER_BLOCK)), jnp.tile(sin, (1, HEADS_PER_BLOCK))


def kernel(x, c, ctx, c_ctx, w_ada, b_ada, norm1_g, w_in, conv_dw, conv_dw_b, conv_ln_g, conv_ln_b, w_conv_out,
           na_rpb, w_na_out, w_out, norm2_g, w_up, ffn_dw, ffn_dw_b, w_down, final_norm_g):
    batch, seq_len, d = x.shape
    depth = w_ada.shape[0]
    ctx_len = ctx.shape[1]
    assert d == D_MODEL and seq_len % GRID_W == 0 and batch + 1 <= MOD_ROWS

    cvec = jnp.zeros((MOD_ROWS, d), F32).at[:batch].set(c).at[batch].set(c_ctx)
    mods, bias, (w_conv_bf, w_na_bf, w_out_bf) = _ada_and_window_bias(
        cvec, w_ada, b_ada, na_rpb, (w_conv_out, w_na_out, w_out))
    mods = mods.reshape(depth, MOD_ROWS, N_MOD, d)
    lat_row = lambda bi: bi
    ctx_row = lambda bi: batch

    cos, sin = _rope_tables(seq_len)
    cos_id = jnp.ones((ctx_len, LANES), F32)
    sin_id = jnp.zeros((ctx_len, LANES), F32)
    rows = lambda v: v.reshape(depth, 1, -1)
    p = {
        "norm1_g": rows(norm1_g), "norm2_g": rows(norm2_g),
        "conv_dw": conv_dw, "conv_dw_b": rows(conv_dw_b), "conv_ln_g": rows(conv_ln_g), "conv_ln_b": rows(conv_ln_b),
        "ffn_dw": ffn_dw, "ffn_dw_b": rows(ffn_dw_b),
        "w_conv_out": w_conv_bf, "w_na_out": w_na_bf, "w_out": w_out_bf,
    }
    fin = final_norm_g.reshape(1, d)

    xc = ctx
    for l in range(depth):
        last = l == depth - 1
        u_c, q_c, k_c, v_c, w_in_l = _inproj(xc, l, mods, ctx_row, p["norm1_g"], cos_id, sin_id, w_in, ctx_len)
        u, q, k, v = _inproj(x, l, mods, lat_row, p["norm1_g"], cos, sin, w_in_l, INPROJ_TM)
        attn, w_up_l, w_down_l = _neighbourhood_attention(q, k, v, k_c, v_c, bias, w_up, w_down, l)
        x = _mix(x, u, attn, l, mods, lat_row, w_in_l, p, MIX_TM)
        x = _ffn(x, l, mods, lat_row, w_up_l, w_down_l, p, fin, last, FFN_TM)
        if not last:
            attn_c = _context_attention(q_c, k_c, v_c)
            xc = _mix(xc, u_c, attn_c, l, mods, ctx_row, w_in_l, p, ctx_len)
            xc = _ffn(xc, l, mods, ctx_row, w_up_l, w_down_l, p, fin, False, ctx_len)
    return x
```

```python
import functools

import jax
import jax.numpy as jnp
from jax import lax
from jax.experimental import pallas as pl
from jax.experimental.pallas import tpu as pltpu

D_MODEL = 1024
GRID_W = 64
CONV_DIM = 512
CONV_K = 31
NA_HEADS = 8
HEAD_DIM = 64
NA_DIM = NA_HEADS * HEAD_DIM
NA_KH = 8
NA_KW = 16
ROPE_THETA = 10000.0
FFN_DIM = 2816
FFN_CONV_K = 3
EPS = 1e-6
GLU_END = 2 * CONV_DIM
QKV_END = GLU_END + 3 * NA_DIM

LANES = 128
SUBLANES = 8
BF16_ROWS = 16
V7X_VMEM_BYTES = 64 * 1024 * 1024
VMEM_LIMIT = V7X_VMEM_BYTES * 7 // 8

HEADS_PER_BLOCK = LANES // HEAD_DIM
N_HEAD_BLOCKS = NA_HEADS // HEADS_PER_BLOCK
CONV_HALO = BF16_ROWS
FFN_HALO = SUBLANES
MOD_ROWS = 8
N_MOD = 6
NA_ROW_GROUP = 8
NA_SLAB = 32
NA_INTERIOR = NA_KH - 1 - NA_KH // 2
NA_SHIFTED = NA_KH
NA_MASKED = NA_KH + 1
N_BIAS_VARIANTS = NA_KH + 2

INPROJ_TM = 1024
MIX_TM = 1024
FFN_TM = 1024
ADA_TK = 256

BF16 = jnp.bfloat16
F32 = jnp.float32


def _params(semantics):
    return pltpu.CompilerParams(dimension_semantics=semantics, vmem_limit_bytes=VMEM_LIMIT)


def _layer_spec(arr, layer):
    tail = (0,) * (arr.ndim - 1)
    return pl.BlockSpec((1,) + arr.shape[1:], lambda *_: (layer,) + tail, pipeline_mode=pl.Buffered(1))


def _whole_spec(arr):
    zeros = (0,) * arr.ndim
    return pl.BlockSpec(arr.shape, lambda *_: zeros, pipeline_mode=pl.Buffered(1))


def _mod_spec(layer, mod_row):
    return pl.BlockSpec((1, 1, N_MOD, D_MODEL), lambda bi, i: (layer, mod_row(bi), 0, 0))


def _rms_modulate(x, gain, shift, scale):
    return x * lax.rsqrt(jnp.mean(x * x, axis=-1, keepdims=True) + EPS) * (gain * (1.0 + scale)) + shift


def _ada_body(c_ref, w_ref, b_ref, o_ref):
    s = jax.nn.silu(c_ref[...]).astype(BF16)
    part = jnp.dot(s, w_ref[0].astype(BF16), preferred_element_type=F32)

    @pl.when(pl.program_id(1) == 0)
    def _():
        o_ref[0] = part + b_ref[0]

    @pl.when(pl.program_id(1) > 0)
    def _():
        o_ref[0] += part


def _ada_bias_body(n_cast, c_ref, w_ref, b_ref, rpb_ref, *refs):
    f32_refs, (o_ref, bias_ref), bf16_refs = refs[:n_cast], refs[n_cast:n_cast + 2], refs[n_cast + 2:]
    _ada_body(c_ref, w_ref, b_ref, o_ref)
    _bias_body(rpb_ref, bias_ref)
    for src, dst in zip(f32_refs, bf16_refs):
        dst[...] = src[...].astype(BF16)


def _ada_and_window_bias(cvec, w_ada, b_ada, rpb, cast_weights):
    depth, d, n = w_ada.shape
    steps = d // ADA_TK
    assert steps == N_HEAD_BLOCKS
    tile = (HEADS_PER_BLOCK * GRID_W, NA_KH * GRID_W)

    def slab(w):
        rows, cols = w.shape[1:]
        assert rows % (steps * BF16_ROWS) == 0
        return pl.BlockSpec((1, rows // steps, cols), lambda l, k: (l, k, 0))

    outs = pl.pallas_call(
        functools.partial(_ada_bias_body, len(cast_weights)),
        grid=(depth, steps),
        in_specs=[
            pl.BlockSpec((MOD_ROWS, ADA_TK), lambda l, k: (0, k)),
            pl.BlockSpec((1, ADA_TK, n), lambda l, k: (l, k, 0)),
            pl.BlockSpec((1, 1, n), lambda l, k: (l, 0, 0)),
            pl.BlockSpec(memory_space=pltpu.SMEM),
        ] + [slab(w) for w in cast_weights],
        out_specs=[pl.BlockSpec((1, MOD_ROWS, n), lambda l, k: (l, 0, 0)),
                   pl.BlockSpec((1, N_BIAS_VARIANTS, 1) + tile, lambda l, hb: (l, 0, hb, 0, 0))]
        + [slab(w) for w in cast_weights],
        out_shape=[jax.ShapeDtypeStruct((depth, MOD_ROWS, n), F32),
                   jax.ShapeDtypeStruct((depth, N_BIAS_VARIANTS, N_HEAD_BLOCKS) + tile, F32)]
        + [jax.ShapeDtypeStruct(w.shape, BF16) for w in cast_weights],
        compiler_params=_params(("arbitrary", "arbitrary")),
        name="ada_bias",
    )(cvec, w_ada, b_ada.reshape(depth, 1, n), rpb.reshape(-1), *cast_weights)
    return outs[0], outs[1], outs[2:]


def _bias_body(rpb_ref, o_ref):
    layer = pl.program_id(0)
    hb = pl.program_id(1)
    c = lax.broadcasted_iota(jnp.int32, (GRID_W, GRID_W), 0)
    cp = lax.broadcasted_iota(jnp.int32, (GRID_W, GRID_W), 1)
    start = jnp.clip(c - NA_KW // 2, 0, GRID_W - NA_KW)
    inside = (cp >= start) & (cp < start + NA_KW)
    diff = cp - c + (NA_KW - 1)
    n_dr = 2 * NA_KH - 1
    n_dc = 2 * NA_KW - 1
    for hh in range(HEADS_PER_BLOCK):
        h = (layer * N_HEAD_BLOCKS + hb) * HEADS_PER_BLOCK + hh
        for i in range(n_dr):
            val = jnp.zeros((GRID_W, GRID_W), F32)
            for d in range(n_dc):
                val = jnp.where(diff == d, rpb_ref[(h * n_dr + i) * n_dc + d], val)
            val = jnp.where(inside, val, -jnp.inf)
            for v in range(NA_KH):
                j = i - v
                if 0 <= j < NA_KH:
                    o_ref[0, v, 0, hh * GRID_W:(hh + 1) * GRID_W, j * GRID_W:(j + 1) * GRID_W] = val
            j = i - (NA_INTERIOR + 1)
            if 0 <= j < NA_KH - 1:
                o_ref[0, NA_SHIFTED, 0, hh * GRID_W:(hh + 1) * GRID_W, j * GRID_W:(j + 1) * GRID_W] = val
        masked = jnp.full((GRID_W, GRID_W), -jnp.inf, F32)
        o_ref[0, NA_SHIFTED, 0, hh * GRID_W:(hh + 1) * GRID_W, (NA_KH - 1) * GRID_W:] = masked
        for j in range(NA_KH):
            o_ref[0, NA_MASKED, 0, hh * GRID_W:(hh + 1) * GRID_W, j * GRID_W:(j + 1) * GRID_W] = masked


def _rope(t, cos, sin, swap_up):
    partner = jnp.where(swap_up, pltpu.roll(t, LANES - HEAD_DIM // 4, 1), pltpu.roll(t, HEAD_DIM // 4, 1))
    return t * cos + partner * sin


def _inproj_body(cast_weights, x_ref, mod_ref, g_ref, cos_ref, sin_ref, win_ref, u_ref, q_ref, k_ref, v_ref,
                 *wb_refs):
    if cast_weights:
        (w_ref,) = wb_refs

        @pl.when((pl.program_id(0) == 0) & (pl.program_id(1) == 0))
        def _():
            w_ref[...] = win_ref[0].astype(BF16)
    else:
        w_ref = win_ref
    mod = mod_ref[0, 0]
    h = _rms_modulate(x_ref[0], g_ref[0], mod[0:1], mod[1:2]).astype(BF16)
    pg = jnp.dot(h, w_ref[:, :GLU_END], preferred_element_type=F32)
    u_ref[0] = (pg[:, :CONV_DIM] * jax.nn.sigmoid(pg[:, CONV_DIM:])).astype(BF16)
    pq = jnp.dot(h, w_ref[:, GLU_END:QKV_END], preferred_element_type=F32)
    cos = cos_ref[...]
    sin = sin_ref[...]
    lane = lax.broadcasted_iota(jnp.int32, cos.shape, 1)
    swap_up = (lane % (HEAD_DIM // 2)) < (HEAD_DIM // 4)
    for blk in range(N_HEAD_BLOCKS):
        sl = slice(blk * LANES, (blk + 1) * LANES)
        q_ref[0, :, sl] = (_rope(pq[:, sl], cos, sin, swap_up) * (HEAD_DIM ** -0.5)).astype(BF16)
        ks = slice(NA_DIM + blk * LANES, NA_DIM + (blk + 1) * LANES)
        k_ref[0, :, sl] = _rope(pq[:, ks], cos, sin, swap_up).astype(BF16)
    v_ref[0] = pq[:, 2 * NA_DIM:].astype(BF16)


def _inproj(x, layer, mods, mod_row, gain, cos, sin, w_in, tm):
    b, t, d = x.shape
    cast_weights = w_in.ndim == 3
    seq_spec = lambda width: pl.BlockSpec((1, tm, width), lambda bi, i: (bi, i, 0))
    out = jax.ShapeDtypeStruct((b, t, NA_DIM), BF16)
    out_specs = [seq_spec(CONV_DIM), seq_spec(NA_DIM), seq_spec(NA_DIM), seq_spec(NA_DIM)]
    out_shape = [jax.ShapeDtypeStruct((b, t, CONV_DIM), BF16), out, out, out]
    if cast_weights:
        out_specs.append(pl.BlockSpec(w_in.shape[1:], lambda bi, i: (0, 0)))
        out_shape.append(jax.ShapeDtypeStruct(w_in.shape[1:], BF16))
    return pl.pallas_call(
        functools.partial(_inproj_body, cast_weights),
        grid=(b, t // tm),
        in_specs=[
            seq_spec(d),
            _mod_spec(layer, mod_row),
            _layer_spec(gain, layer),
            pl.BlockSpec((tm, LANES), lambda bi, i: (i, 0)),
            pl.BlockSpec((tm, LANES), lambda bi, i: (i, 0)),
            _layer_spec(w_in, layer) if cast_weights else _whole_spec(w_in),
        ],
        out_specs=out_specs,
        out_shape=out_shape,
        compiler_params=_params(("arbitrary", "arbitrary")),
        name="inproj",
    )(x, mods, gain, cos, sin, w_in)


def _stack_heads(t):
    lane = lax.broadcasted_iota(jnp.int32, t.shape, 1)
    zero = jnp.zeros_like(t)
    return jnp.concatenate([jnp.where(lane < HEAD_DIM, t, zero), jnp.where(lane >= HEAD_DIM, t, zero)], axis=0)


def _unstack_heads(o, rows):
    lane = lax.broadcasted_iota(jnp.int32, (rows, LANES), 1)
    return jnp.where(lane < HEAD_DIM, o[:rows], o[rows:])


def _scores(lhs, keys):
    return lax.dot_general(lhs, keys, (((1,), (1,)), ((), ())), preferred_element_type=F32)


def _na_body(q_ref, k_ref, v_ref, kc_ref, vc_ref, bias_ref, wup_ref, wdown_ref, o_ref, wup_bf_ref, wdown_bf_ref):
    wup_bf_ref[...] = wup_ref[0].astype(BF16)
    wdown_bf_ref[...] = wdown_ref[0].astype(BF16)
    rows = q_ref.shape[1] // GRID_W
    band = NA_KH * GRID_W
    kc = kc_ref[0]
    vc = vc_ref[0]

    def softmax(scores, biases):
        probs = [[] for _ in scores]
        inv = []
        for t in range(2 * GRID_W // NA_SLAB):
            sl = slice(t * NA_SLAB, (t + 1) * NA_SLAB)
            s = [a[sl] if bi is None else a[sl] + bi[sl] for a, bi in zip(scores, biases)]
            m = functools.reduce(jnp.maximum, [jnp.max(a, axis=-1, keepdims=True) for a in s])
            e = [jnp.exp(a - m) for a in s]
            inv.append(1.0 / sum(jnp.sum(a, axis=-1, keepdims=True) for a in e))
            for dst, a in zip(probs, e):
                dst.append(a.astype(BF16))
        return [jnp.concatenate(pp, axis=0) for pp in probs], jnp.concatenate(inv, axis=0)

    def group(g, carry):
        pairs = [g * NA_ROW_GROUP + 2 * j for j in range(NA_ROW_GROUP // 2)]
        start_b = [jnp.clip(r + 1 - NA_KH // 2, 0, rows - NA_KH) for r in pairs]
        start_a = [jnp.clip(r - NA_KH // 2, 0, rows - NA_KH) for r in pairs]
        shift = [sb - sa for sa, sb in zip(start_a, start_b)]
        q0 = [pl.multiple_of(r * GRID_W, 2 * GRID_W) for r in pairs]
        k0 = [pl.multiple_of(s * GRID_W, GRID_W) for s in start_b]
        e0 = [pl.multiple_of(s * GRID_W, GRID_W) for s in start_a]
        var_b = [sb - (r + 1) + (NA_KH - 1) for sb, r in zip(start_b, pairs)]
        var_a = [jnp.where(sh == 1, NA_SHIFTED, sa - r + (NA_KH - 1)) for sh, sa, r in zip(shift, start_a, pairs)]
        var_e = [jnp.where(sh == 1, NA_INTERIOR, NA_MASKED) for sh in shift]
        lhs_a = [_stack_heads(q_ref[0, pl.ds(a, GRID_W), :]) for a in q0]
        lhs_b = [_stack_heads(q_ref[0, pl.ds(a + GRID_W, GRID_W), :]) for a in q0]
        lhs = [jnp.concatenate([a, b2], axis=0) for a, b2 in zip(lhs_a, lhs_b)]
        s_win = [_scores(l, k_ref[0, pl.ds(a, band), :]) for l, a in zip(lhs, k0)]
        s_ext = [_scores(l, k_ref[0, pl.ds(a, GRID_W), :]) for l, a in zip(lhs_a, e0)]
        s_ctx = [_scores(l, kc) for l in lhs]
        half = 2 * GRID_W
        for j in range(NA_ROW_GROUP // 2):
            (pw_a, pe_a, pc_a), inv_a = softmax(
                [s_win[j][:half], s_ext[j], s_ctx[j][:half]],
                [bias_ref[0, var_a[j], 0], bias_ref[0, var_e[j], 0, :, :GRID_W], None])
            (pw_b, pc_b), inv_b = softmax([s_win[j][half:], s_ctx[j][half:]], [bias_ref[0, var_b[j], 0], None])
            o = jnp.dot(jnp.concatenate([pw_a, pw_b], axis=0), v_ref[0, pl.ds(k0[j], band), :],
                        preferred_element_type=F32)
            o = o + jnp.dot(jnp.concatenate([pc_a, pc_b], axis=0), vc, preferred_element_type=F32)
            o_a = o[:half] + jnp.dot(pe_a, v_ref[0, pl.ds(e0[j], GRID_W), :], preferred_element_type=F32)
            o_ref[0, pl.ds(q0[j], GRID_W), :] = _unstack_heads(o_a * inv_a, GRID_W).astype(BF16)
            o_ref[0, pl.ds(q0[j] + GRID_W, GRID_W), :] = _unstack_heads(o[half:] * inv_b, GRID_W).astype(BF16)
        return carry

    lax.fori_loop(0, rows // NA_ROW_GROUP, group, 0)


def _neighbourhood_attention(q, k, v, kc, vc, bias, w_up, w_down, layer):
    b, t, _ = q.shape
    ctx = kc.shape[1]
    steps = b * N_HEAD_BLOCKS
    assert (t // GRID_W) % NA_ROW_GROUP == 0
    seq = lambda length: pl.BlockSpec((1, length, LANES), lambda bi, hb: (bi, 0, hb))

    def slab_specs(w):
        rows, cols = w.shape[1:]
        assert rows % (steps * BF16_ROWS) == 0
        slab = rows // steps
        return (pl.BlockSpec((1, slab, cols), lambda bi, hb: (layer, bi * N_HEAD_BLOCKS + hb, 0)),
                pl.BlockSpec((slab, cols), lambda bi, hb: (bi * N_HEAD_BLOCKS + hb, 0)),
                jax.ShapeDtypeStruct((rows, cols), BF16))

    up_in, up_out, up_shape = slab_specs(w_up)
    down_in, down_out, down_shape = slab_specs(w_down)
    return pl.pallas_call(
        _na_body,
        grid=(b, N_HEAD_BLOCKS),
        in_specs=[seq(t), seq(t), seq(t), seq(ctx), seq(ctx),
                  pl.BlockSpec((1, N_BIAS_VARIANTS, 1) + bias.shape[3:], lambda bi, hb: (layer, 0, hb, 0, 0)),
                  up_in, down_in],
        out_specs=[seq(t), up_out, down_out],
        out_shape=[jax.ShapeDtypeStruct((b, t, NA_DIM), BF16), up_shape, down_shape],
        compiler_params=_params(("arbitrary", "arbitrary")),
        name="na_attention",
    )(q, k, v, kc, vc, bias, w_up, w_down)


def _ctx_attn_body(q_ref, k_ref, v_ref, o_ref):
    n = q_ref.shape[1]
    lhs = _stack_heads(q_ref[0])
    s = _scores(lhs, k_ref[0])
    p = jnp.exp(s - jnp.max(s, axis=-1, keepdims=True))
    denom = jnp.sum(p, axis=-1, keepdims=True)
    o = jnp.dot(p.astype(BF16), v_ref[0], preferred_element_type=F32)
    o_ref[0] = _unstack_heads(o / denom, n).astype(BF16)


def _context_attention(q, k, v):
    b, n, _ = q.shape
    spec = pl.BlockSpec((1, n, LANES), lambda bi, hb: (bi, 0, hb))
    return pl.pallas_call(
        _ctx_attn_body,
        grid=(b, N_HEAD_BLOCKS),
        in_specs=[spec, spec, spec],
        out_specs=spec,
        out_shape=jax.ShapeDtypeStruct((b, n, NA_DIM), BF16),
        compiler_params=_params(("arbitrary", "arbitrary")),
        name="ctx_attention",
    )(q, k, v)


def _mix_body(x_ref, up_ref, uc_ref, un_ref, a_ref, mod_ref, win_ref, g_ref, dw_ref, dwb_ref, lng_ref, lnb_ref,
              wconv_ref, wna_ref, wout_ref, o_ref):
    i = pl.program_id(1)
    tm = x_ref.shape[1]
    x = x_ref[0]
    mod = mod_ref[0, 0]
    h = _rms_modulate(x, g_ref[0], mod[0:1], mod[1:2]).astype(BF16)
    gates = jax.nn.sigmoid(jnp.dot(h, win_ref[:, QKV_END:], preferred_element_type=F32))
    y_attn = jnp.dot(a_ref[0], wna_ref[0], preferred_element_type=F32)
    prev = up_ref[0].astype(F32)
    nxt = un_ref[0].astype(F32)
    prev = jnp.where(i > 0, prev, jnp.zeros_like(prev))
    nxt = jnp.where(i < pl.num_programs(1) - 1, nxt, jnp.zeros_like(nxt))
    ucat = jnp.concatenate([prev, uc_ref[0].astype(F32), nxt], axis=0)
    n = tm + 2 * CONV_HALO
    assert CONV_HALO - CONV_K // 2 == 1 and CONV_K < 4 * SUBLANES
    acc = jnp.zeros((tm, CONV_DIM), F32) + dwb_ref[0]
    for b in range(SUBLANES):
        shifted = ucat if b == 0 else pltpu.roll(ucat, n - b, 0)
        for a in range(4):
            j = SUBLANES * a + b
            if 1 <= j <= CONV_K:
                acc = acc + dw_ref[0, j - 1:j, :] * shifted[SUBLANES * a:SUBLANES * a + tm]
    mu = jnp.mean(acc, axis=-1, keepdims=True)
    cen = acc - mu
    var = jnp.mean(cen * cen, axis=-1, keepdims=True)
    feat = jax.nn.silu(cen * lax.rsqrt(var + EPS) * lng_ref[0] + lnb_ref[0]).astype(BF16)
    y_conv = jnp.dot(feat, wconv_ref[0], preferred_element_type=F32)
    merged = (gates[:, :D_MODEL] * y_conv + gates[:, D_MODEL:] * y_attn).astype(BF16)
    o_ref[0] = x + mod[2:3] * jnp.dot(merged, wout_ref[0], preferred_element_type=F32)


def _mix(x, u, attn, layer, mods, mod_row, w_in, p, tm):
    b, t, d = x.shape
    per = tm // CONV_HALO
    last = t // CONV_HALO - 1
    seq = lambda width: pl.BlockSpec((1, tm, width), lambda bi, i: (bi, i, 0))
    weights = (p["norm1_g"], p["conv_dw"], p["conv_dw_b"], p["conv_ln_g"], p["conv_ln_b"],
               p["w_conv_out"], p["w_na_out"], p["w_out"])
    return pl.pallas_call(
        _mix_body,
        grid=(b, t // tm),
        in_specs=[
            seq(d),
            pl.BlockSpec((1, CONV_HALO, CONV_DIM), lambda bi, i: (bi, jnp.maximum(i * per - 1, 0), 0)),
            seq(CONV_DIM),
            pl.BlockSpec((1, CONV_HALO, CONV_DIM), lambda bi, i: (bi, jnp.minimum((i + 1) * per, last), 0)),
            seq(NA_DIM),
            _mod_spec(layer, mod_row),
            _whole_spec(w_in),
        ] + [_layer_spec(w, layer) for w in weights],
        out_specs=seq(d),
        out_shape=jax.ShapeDtypeStruct((b, t, d), F32),
        compiler_params=_params(("arbitrary", "arbitrary")),
        name="mix",
    )(x, u, u, u, attn, mods, w_in, *weights)


def _ffn_body(final_norm, x_ref, xp_ref, xn_ref, mod_ref, wup_ref, wdown_ref, g_ref, dw_ref, dwb_ref, fg_ref,
              o_ref):
    i = pl.program_id(1)
    tm = x_ref.shape[1]
    mod = mod_ref[0, 0]
    gain = g_ref[0]
    x = x_ref[0]
    hp = _rms_modulate(xp_ref[0], gain, mod[3:4], mod[4:5])
    hn = _rms_modulate(xn_ref[0], gain, mod[3:4], mod[4:5])
    hp = jnp.where(i > 0, hp, jnp.zeros_like(hp))
    hn = jnp.where(i < pl.num_programs(1) - 1, hn, jnp.zeros_like(hn))
    h = jnp.concatenate([hp, _rms_modulate(x, gain, mod[3:4], mod[4:5]), hn], axis=0).astype(BF16)
    n = tm + 2 * FFN_HALO
    assert FFN_CONV_K == 3

    def conv(cols, up):
        before = pltpu.roll(up, 1, 0)[FFN_HALO:FFN_HALO + tm]
        after = pltpu.roll(up, n - 1, 0)[FFN_HALO:FFN_HALO + tm]
        return (dwb_ref[0, :, cols] + dw_ref[0, 0:1, cols] * before
                + dw_ref[0, 1:2, cols] * up[FFN_HALO:FFN_HALO + tm] + dw_ref[0, 2:3, cols] * after)

    value_cols = slice(0, FFN_DIM)
    gate_cols = slice(FFN_DIM, 2 * FFN_DIM)
    up_gate = jnp.dot(h, wup_ref[:, gate_cols], preferred_element_type=F32)
    up_value = jnp.dot(h, wup_ref[:, value_cols], preferred_element_type=F32)
    act = (jax.nn.silu(conv(gate_cols, up_gate)) * conv(value_cols, up_value)).astype(BF16)
    y = x + mod[5:6] * jnp.dot(act, wdown_ref[...], preferred_element_type=F32)
    if final_norm:
        y = y * lax.rsqrt(jnp.mean(y * y, axis=-1, keepdims=True) + EPS) * fg_ref[...]
    o_ref[0] = y


def _ffn(x, layer, mods, mod_row, w_up, w_down, p, final_gain, final_norm, tm):
    b, t, d = x.shape
    per = tm // FFN_HALO
    last = t // FFN_HALO - 1
    seq = pl.BlockSpec((1, tm, d), lambda bi, i: (bi, i, 0))
    weights = (p["norm2_g"], p["ffn_dw"], p["ffn_dw_b"])
    return pl.pallas_call(
        functools.partial(_ffn_body, final_norm),
        grid=(b, t // tm),
        in_specs=[
            seq,
            pl.BlockSpec((1, FFN_HALO, d), lambda bi, i: (bi, jnp.maximum(i * per - 1, 0), 0)),
            pl.BlockSpec((1, FFN_HALO, d), lambda bi, i: (bi, jnp.minimum((i + 1) * per, last), 0)),
            _mod_spec(layer, mod_row),
            _whole_spec(w_up),
            _whole_spec(w_down),
        ] + [_layer_spec(w, layer) for w in weights] + [_whole_spec(final_gain)],
        out_specs=seq,
        out_shape=jax.ShapeDtypeStruct((b, t, d), F32),
        compiler_params=_params(("arbitrary", "arbitrary")),
        name="ffn",
    )(x, x, x, mods, w_up, w_down, *weights, final_gain)


def _rope_tables(seq_len):
    half = HEAD_DIM // 2
    inv_freq = ROPE_THETA ** (-jnp.arange(0, half, 2, dtype=F32) / half)
    t = jnp.arange(seq_len)
    ang_r = (t // GRID_W).astype(F32)[:, None] * inv_freq[None, :]
    ang_c = (t % GRID_W).astype(F32)[:, None] * inv_freq[None, :]
    cos = jnp.concatenate([jnp.cos(ang_r)] * 2 + [jnp.cos(ang_c)] * 2, axis=-1)
    sin = jnp.concatenate([-jnp.sin(ang_r), jnp.sin(ang_r), -jnp.sin(ang_c), jnp.sin(ang_c)], axis=-1)
    return jnp.tile(cos, (1, HEADS_PER_BLOCK)), jnp.tile(sin, (1, HEADS_PER_BLOCK))


def kernel(x, c, ctx, c_ctx, w_ada, b_ada, norm1_g, w_in, conv_dw, conv_dw_b, conv_ln_g, conv_ln_b, w_conv_out,
           na_rpb, w_na_out, w_out, norm2_g, w_up, ffn_dw, ffn_dw_b, w_down, final_norm_g):
    batch, seq_len, d = x.shape
    depth = w_ada.shape[0]
    ctx_len = ctx.shape[1]
    assert d == D_MODEL and seq_len % GRID_W == 0 and batch + 1 <= MOD_ROWS

    cvec = jnp.zeros((MOD_ROWS, d), F32).at[:batch].set(c).at[batch].set(c_ctx)
    mods, bias, (w_conv_bf, w_na_bf, w_out_bf) = _ada_and_window_bias(
        cvec, w_ada, b_ada, na_rpb, (w_conv_out, w_na_out, w_out))
    mods = mods.reshape(depth, MOD_ROWS, N_MOD, d)
    lat_row = lambda bi: bi
    ctx_row = lambda bi: batch

    cos, sin = _rope_tables(seq_len)
    cos_id = jnp.ones((ctx_len, LANES), F32)
    sin_id = jnp.zeros((ctx_len, LANES), F32)
    rows = lambda v: v.reshape(depth, 1, -1)
    p = {
        "norm1_g": rows(norm1_g), "norm2_g": rows(norm2_g),
        "conv_dw": conv_dw, "conv_dw_b": rows(conv_dw_b), "conv_ln_g": rows(conv_ln_g), "conv_ln_b": rows(conv_ln_b),
        "ffn_dw": ffn_dw, "ffn_dw_b": rows(ffn_dw_b),
        "w_conv_out": w_conv_bf, "w_na_out": w_na_bf, "w_out": w_out_bf,
    }
    fin = final_norm_g.reshape(1, d)

    xc = ctx
    for l in range(depth):
        last = l == depth - 1
        u_c, q_c, k_c, v_c, w_in_l = _inproj(xc, l, mods, ctx_row, p["norm1_g"], cos_id, sin_id, w_in, ctx_len)
        u, q, k, v = _inproj(x, l, mods, lat_row, p["norm1_g"], cos, sin, w_in_l, INPROJ_TM)
        attn, w_up_l, w_down_l = _neighbourhood_attention(q, k, v, k_c, v_c, bias, w_up, w_down, l)
        x = _mix(x, u, attn, l, mods, lat_row, w_in_l, p, MIX_TM)
        x = _ffn(x, l, mods, lat_row, w_up_l, w_down_l, p, fin, last, FFN_TM)
        if not last:
            attn_c = _context_attention(q_c, k_c, v_c)
            xc = _mix(xc, u_c, attn_c, l, mods, ctx_row, w_in_l, p, ctx_len)
            xc = _ffn(xc, l, mods, ctx_row, w_up_l, w_down_l, p, fin, False, ctx_len)
    return x
```

```python
import functools

import jax
import jax.numpy as jnp
from jax import lax
from jax.experimental import pallas as pl
from jax.experimental.pallas import tpu as pltpu

D_MODEL = 1024
GRID_W = 64
CONV_DIM = 512
CONV_K = 31
NA_HEADS = 8
HEAD_DIM = 64
NA_DIM = NA_HEADS * HEAD_DIM
NA_KH = 8
NA_KW = 16
ROPE_THETA = 10000.0
FFN_DIM = 2816
FFN_CONV_K = 3
EPS = 1e-6
GLU_END = 2 * CONV_DIM
QKV_END = GLU_END + 3 * NA_DIM

LANES = 128
SUBLANES = 8
BF16_ROWS = 16
V7X_VMEM_BYTES = 64 * 1024 * 1024
VMEM_LIMIT = V7X_VMEM_BYTES * 7 // 8

HEADS_PER_BLOCK = LANES // HEAD_DIM
N_HEAD_BLOCKS = NA_HEADS // HEADS_PER_BLOCK
CONV_HALO = BF16_ROWS
FFN_HALO = SUBLANES
MOD_ROWS = 8
N_MOD = 6
NA_ROW_GROUP = 8
NA_SLAB = 32

INPROJ_TM = 1024
MIX_TM = 1024
MIX_SPLIT = 2
FFN_TM = 1024
ADA_TK = 256

BF16 = jnp.bfloat16
F32 = jnp.float32


def _params(semantics):
    return pltpu.CompilerParams(dimension_semantics=semantics, vmem_limit_bytes=VMEM_LIMIT)


def _layer_spec(arr, layer):
    tail = (0,) * (arr.ndim - 1)
    return pl.BlockSpec((1,) + arr.shape[1:], lambda *_: (layer,) + tail, pipeline_mode=pl.Buffered(1))


def _whole_spec(arr):
    zeros = (0,) * arr.ndim
    return pl.BlockSpec(arr.shape, lambda *_: zeros, pipeline_mode=pl.Buffered(1))


def _mod_spec(layer, mod_row):
    return pl.BlockSpec((1, 1, N_MOD, D_MODEL), lambda bi, i: (layer, mod_row(bi), 0, 0))


def _rms_modulate(x, gain, shift, scale):
    return x * lax.rsqrt(jnp.mean(x * x, axis=-1, keepdims=True) + EPS) * (gain * (1.0 + scale)) + shift


def _ada_body(c_ref, w_ref, b_ref, o_ref):
    s = jax.nn.silu(c_ref[...]).astype(BF16)
    part = jnp.dot(s, w_ref[0].astype(BF16), preferred_element_type=F32)

    @pl.when(pl.program_id(1) == 0)
    def _():
        o_ref[0] = part + b_ref[0]

    @pl.when(pl.program_id(1) > 0)
    def _():
        o_ref[0] += part


def _ada_bias_body(n_cast, c_ref, w_ref, b_ref, rpb_ref, *refs):
    f32_refs, (o_ref, bias_ref), bf16_refs = refs[:n_cast], refs[n_cast:n_cast + 2], refs[n_cast + 2:]
    _ada_body(c_ref, w_ref, b_ref, o_ref)
    _bias_body(rpb_ref, bias_ref)
    for src, dst in zip(f32_refs, bf16_refs):
        dst[...] = src[...].astype(BF16)


def _ada_and_window_bias(cvec, w_ada, b_ada, rpb, cast_weights):
    depth, d, n = w_ada.shape
    steps = d // ADA_TK
    assert steps == N_HEAD_BLOCKS
    tile = (HEADS_PER_BLOCK * GRID_W, NA_KH * GRID_W)

    def slab(w):
        rows, cols = w.shape[1:]
        assert rows % (steps * BF16_ROWS) == 0
        return pl.BlockSpec((1, rows // steps, cols), lambda l, k: (l, k, 0))

    outs = pl.pallas_call(
        functools.partial(_ada_bias_body, len(cast_weights)),
        grid=(depth, steps),
        in_specs=[
            pl.BlockSpec((MOD_ROWS, ADA_TK), lambda l, k: (0, k)),
            pl.BlockSpec((1, ADA_TK, n), lambda l, k: (l, k, 0)),
            pl.BlockSpec((1, 1, n), lambda l, k: (l, 0, 0)),
            pl.BlockSpec(memory_space=pltpu.SMEM),
        ] + [slab(w) for w in cast_weights],
        out_specs=[pl.BlockSpec((1, MOD_ROWS, n), lambda l, k: (l, 0, 0)),
                   pl.BlockSpec((1, NA_KH, 1) + tile, lambda l, hb: (l, 0, hb, 0, 0))]
        + [slab(w) for w in cast_weights],
        out_shape=[jax.ShapeDtypeStruct((depth, MOD_ROWS, n), F32),
                   jax.ShapeDtypeStruct((depth, NA_KH, N_HEAD_BLOCKS) + tile, F32)]
        + [jax.ShapeDtypeStruct(w.shape, BF16) for w in cast_weights],
        compiler_params=_params(("arbitrary", "arbitrary")),
        name="ada_bias",
    )(cvec, w_ada, b_ada.reshape(depth, 1, n), rpb.reshape(-1), *cast_weights)
    return outs[0], outs[1], outs[2:]


def _bias_body(rpb_ref, o_ref):
    layer = pl.program_id(0)
    hb = pl.program_id(1)
    c = lax.broadcasted_iota(jnp.int32, (GRID_W, GRID_W), 0)
    cp = lax.broadcasted_iota(jnp.int32, (GRID_W, GRID_W), 1)
    start = jnp.clip(c - NA_KW // 2, 0, GRID_W - NA_KW)
    inside = (cp >= start) & (cp < start + NA_KW)
    diff = cp - c + (NA_KW - 1)
    n_dr = 2 * NA_KH - 1
    n_dc = 2 * NA_KW - 1
    for hh in range(HEADS_PER_BLOCK):
        h = (layer * N_HEAD_BLOCKS + hb) * HEADS_PER_BLOCK + hh
        for i in range(n_dr):
            val = jnp.zeros((GRID_W, GRID_W), F32)
            for d in range(n_dc):
                val = jnp.where(diff == d, rpb_ref[(h * n_dr + i) * n_dc + d], val)
            val = jnp.where(inside, val, -jnp.inf)
            for v in range(NA_KH):
                j = i - v
                if 0 <= j < NA_KH:
                    o_ref[0, v, 0, hh * GRID_W:(hh + 1) * GRID_W, j * GRID_W:(j + 1) * GRID_W] = val


def _rope(t, cos, sin, swap_up):
    partner = jnp.where(swap_up, pltpu.roll(t, LANES - HEAD_DIM // 4, 1), pltpu.roll(t, HEAD_DIM // 4, 1))
    return t * cos + partner * sin


def _inproj_body(cast_weights, x_ref, mod_ref, g_ref, cos_ref, sin_ref, win_ref, u_ref, q_ref, k_ref, v_ref,
                 *wb_refs):
    if cast_weights:
        (w_ref,) = wb_refs

        @pl.when((pl.program_id(0) == 0) & (pl.program_id(1) == 0))
        def _():
            w_ref[...] = win_ref[0].astype(BF16)
    else:
        w_ref = win_ref
    mod = mod_ref[0, 0]
    h = _rms_modulate(x_ref[0], g_ref[0], mod[0:1], mod[1:2]).astype(BF16)
    pg = jnp.dot(h, w_ref[:, :GLU_END], preferred_element_type=F32)
    u_ref[0] = (pg[:, :CONV_DIM] * jax.nn.sigmoid(pg[:, CONV_DIM:])).astype(BF16)
    pq = jnp.dot(h, w_ref[:, GLU_END:QKV_END], preferred_element_type=F32)
    cos = cos_ref[...]
    sin = sin_ref[...]
    lane = lax.broadcasted_iota(jnp.int32, cos.shape, 1)
    swap_up = (lane % (HEAD_DIM // 2)) < (HEAD_DIM // 4)
    for blk in range(N_HEAD_BLOCKS):
        sl = slice(blk * LANES, (blk + 1) * LANES)
        q_ref[0, :, sl] = (_rope(pq[:, sl], cos, sin, swap_up) * (HEAD_DIM ** -0.5)).astype(BF16)
        ks = slice(NA_DIM + blk * LANES, NA_DIM + (blk + 1) * LANES)
        k_ref[0, :, sl] = _rope(pq[:, ks], cos, sin, swap_up).astype(BF16)
    v_ref[0] = pq[:, 2 * NA_DIM:].astype(BF16)


def _inproj(x, layer, mods, mod_row, gain, cos, sin, w_in, tm):
    b, t, d = x.shape
    cast_weights = w_in.ndim == 3
    seq_spec = lambda width: pl.BlockSpec((1, tm, width), lambda bi, i: (bi, i, 0))
    out = jax.ShapeDtypeStruct((b, t, NA_DIM), BF16)
    out_specs = [seq_spec(CONV_DIM), seq_spec(NA_DIM), seq_spec(NA_DIM), seq_spec(NA_DIM)]
    out_shape = [jax.ShapeDtypeStruct((b, t, CONV_DIM), BF16), out, out, out]
    if cast_weights:
        out_specs.append(pl.BlockSpec(w_in.shape[1:], lambda bi, i: (0, 0)))
        out_shape.append(jax.ShapeDtypeStruct(w_in.shape[1:], BF16))
    return pl.pallas_call(
        functools.partial(_inproj_body, cast_weights),
        grid=(b, t // tm),
        in_specs=[
            seq_spec(d),
            _mod_spec(layer, mod_row),
            _layer_spec(gain, layer),
            pl.BlockSpec((tm, LANES), lambda bi, i: (i, 0)),
            pl.BlockSpec((tm, LANES), lambda bi, i: (i, 0)),
            _layer_spec(w_in, layer) if cast_weights else _whole_spec(w_in),
        ],
        out_specs=out_specs,
        out_shape=out_shape,
        compiler_params=_params(("arbitrary", "arbitrary")),
        name="inproj",
    )(x, mods, gain, cos, sin, w_in)


def _stack_heads(t):
    lane = lax.broadcasted_iota(jnp.int32, t.shape, 1)
    zero = jnp.zeros_like(t)
    return jnp.concatenate([jnp.where(lane < HEAD_DIM, t, zero), jnp.where(lane >= HEAD_DIM, t, zero)], axis=0)


def _unstack_heads(o, rows):
    lane = lax.broadcasted_iota(jnp.int32, (rows, LANES), 1)
    return jnp.where(lane < HEAD_DIM, o[:rows], o[rows:])


def _scores(lhs, keys):
    return lax.dot_general(lhs, keys, (((1,), (1,)), ((), ())), preferred_element_type=F32)


def _na_body(q_ref, k_ref, v_ref, kc_ref, vc_ref, bias_ref, wup_ref, wdown_ref, o_ref, wup_bf_ref, wdown_bf_ref):
    wup_bf_ref[...] = wup_ref[0].astype(BF16)
    wdown_bf_ref[...] = wdown_ref[0].astype(BF16)
    rows = q_ref.shape[1] // GRID_W
    band = NA_KH * GRID_W
    kc = kc_ref[0]
    vc = vc_ref[0]

    def group(g, carry):
        rs = [g * NA_ROW_GROUP + j for j in range(NA_ROW_GROUP)]
        starts = [jnp.clip(r - NA_KH // 2, 0, rows - NA_KH) for r in rs]
        q0 = [pl.multiple_of(r * GRID_W, GRID_W) for r in rs]
        k0 = [pl.multiple_of(s * GRID_W, GRID_W) for s in starts]
        lhs = [_stack_heads(q_ref[0, pl.ds(a, GRID_W), :]) for a in q0]
        variant = [s - r + (NA_KH - 1) for s, r in zip(starts, rs)]
        s_win = [_scores(l, k_ref[0, pl.ds(a, band), :]) for l, a in zip(lhs, k0)]
        s_ctx = [_scores(l, kc) for l in lhs]
        n_slabs = 2 * GRID_W // NA_SLAB
        for j in range(NA_ROW_GROUP):
            pw, pc, inv = [], [], []
            for t in range(n_slabs):
                sl = slice(t * NA_SLAB, (t + 1) * NA_SLAB)
                sw = s_win[j][sl] + bias_ref[0, variant[j], 0, sl, :]
                sc = s_ctx[j][sl]
                m = jnp.maximum(jnp.max(sw, axis=-1, keepdims=True), jnp.max(sc, axis=-1, keepdims=True))
                ew = jnp.exp(sw - m)
                ec = jnp.exp(sc - m)
                inv.append(1.0 / (jnp.sum(ew, axis=-1, keepdims=True) + jnp.sum(ec, axis=-1, keepdims=True)))
                pw.append(ew.astype(BF16))
                pc.append(ec.astype(BF16))
            o = jnp.dot(jnp.concatenate(pw, axis=0), v_ref[0, pl.ds(k0[j], band), :], preferred_element_type=F32)
            o = o + jnp.dot(jnp.concatenate(pc, axis=0), vc, preferred_element_type=F32)
            o = o * jnp.concatenate(inv, axis=0)
            o_ref[0, pl.ds(q0[j], GRID_W), :] = _unstack_heads(o, GRID_W).astype(BF16)
        return carry

    lax.fori_loop(0, rows // NA_ROW_GROUP, group, 0)


def _neighbourhood_attention(q, k, v, kc, vc, bias, w_up, w_down, layer):
    b, t, _ = q.shape
    ctx = kc.shape[1]
    steps = b * N_HEAD_BLOCKS
    assert (t // GRID_W) % NA_ROW_GROUP == 0
    seq = lambda length: pl.BlockSpec((1, length, LANES), lambda bi, hb: (bi, 0, hb))

    def slab_specs(w):
        rows, cols = w.shape[1:]
        assert rows % (steps * BF16_ROWS) == 0
        slab = rows // steps
        return (pl.BlockSpec((1, slab, cols), lambda bi, hb: (layer, bi * N_HEAD_BLOCKS + hb, 0)),
                pl.BlockSpec((slab, cols), lambda bi, hb: (bi * N_HEAD_BLOCKS + hb, 0)),
                jax.ShapeDtypeStruct((rows, cols), BF16))

    up_in, up_out, up_shape = slab_specs(w_up)
    down_in, down_out, down_shape = slab_specs(w_down)
    return pl.pallas_call(
        _na_body,
        grid=(b, N_HEAD_BLOCKS),
        in_specs=[seq(t), seq(t), seq(t), seq(ctx), seq(ctx),
                  pl.BlockSpec((1, NA_KH, 1) + bias.shape[3:], lambda bi, hb: (layer, 0, hb, 0, 0)),
                  up_in, down_in],
        out_specs=[seq(t), up_out, down_out],
        out_shape=[jax.ShapeDtypeStruct((b, t, NA_DIM), BF16), up_shape, down_shape],
        compiler_params=_params(("arbitrary", "arbitrary")),
        name="na_attention",
    )(q, k, v, kc, vc, bias, w_up, w_down)


def _ctx_attn_body(q_ref, k_ref, v_ref, o_ref):
    n = q_ref.shape[1]
    lhs = _stack_heads(q_ref[0])
    s = _scores(lhs, k_ref[0])
    p = jnp.exp(s - jnp.max(s, axis=-1, keepdims=True))
    denom = jnp.sum(p, axis=-1, keepdims=True)
    o = jnp.dot(p.astype(BF16), v_ref[0], preferred_element_type=F32)
    o_ref[0] = _unstack_heads(o / denom, n).astype(BF16)


def _context_attention(q, k, v):
    b, n, _ = q.shape
    spec = pl.BlockSpec((1, n, LANES), lambda bi, hb: (bi, 0, hb))
    return pl.pallas_call(
        _ctx_attn_body,
        grid=(b, N_HEAD_BLOCKS),
        in_specs=[spec, spec, spec],
        out_specs=spec,
        out_shape=jax.ShapeDtypeStruct((b, n, NA_DIM), BF16),
        compiler_params=_params(("arbitrary", "arbitrary")),
        name="ctx_attention",
    )(q, k, v)


def _mix_body(x_ref, up_ref, uc_ref, un_ref, a_ref, mod_ref, win_ref, g_ref, dw_ref, dwb_ref, lng_ref, lnb_ref,
              wconv_ref, wna_ref, wout_ref, o_ref):
    i = pl.program_id(1)
    tm = x_ref.shape[1]
    sub = tm // MIX_SPLIT
    parts = [slice(s * sub, (s + 1) * sub) for s in range(MIX_SPLIT)]
    mod = mod_ref[0, 0]
    xs = [x_ref[0, r, :] for r in parts]
    hs = [_rms_modulate(x, g_ref[0], mod[0:1], mod[1:2]).astype(BF16) for x in xs]
    gates = [jax.nn.sigmoid(jnp.dot(h, win_ref[:, QKV_END:], preferred_element_type=F32)) for h in hs]
    y_attn = [jnp.dot(a_ref[0, r, :], wna_ref[0], preferred_element_type=F32) for r in parts]
    prev = up_ref[0].astype(F32)
    nxt = un_ref[0].astype(F32)
    prev = jnp.where(i > 0, prev, jnp.zeros_like(prev))
    nxt = jnp.where(i < pl.num_programs(1) - 1, nxt, jnp.zeros_like(nxt))
    ucat = jnp.concatenate([prev, uc_ref[0].astype(F32), nxt], axis=0)
    n = sub + 2 * CONV_HALO
    assert CONV_HALO - CONV_K // 2 == 1 and CONV_K < 4 * SUBLANES
    feats = []
    for s in range(MIX_SPLIT):
        window = ucat[s * sub:s * sub + n]
        acc = jnp.zeros((sub, CONV_DIM), F32) + dwb_ref[0]
        for b in range(SUBLANES):
            shifted = window if b == 0 else pltpu.roll(window, n - b, 0)
            for a in range(4):
                j = SUBLANES * a + b
                if 1 <= j <= CONV_K:
                    acc = acc + dw_ref[0, j - 1:j, :] * shifted[SUBLANES * a:SUBLANES * a + sub]
        mu = jnp.mean(acc, axis=-1, keepdims=True)
        cen = acc - mu
        var = jnp.mean(cen * cen, axis=-1, keepdims=True)
        feats.append(jax.nn.silu(cen * lax.rsqrt(var + EPS) * lng_ref[0] + lnb_ref[0]).astype(BF16))
    y_conv = [jnp.dot(f, wconv_ref[0], preferred_element_type=F32) for f in feats]
    merged = [(g[:, :D_MODEL] * yc + g[:, D_MODEL:] * ya).astype(BF16) for g, yc, ya in zip(gates, y_conv, y_attn)]
    for r, x, m in zip(parts, xs, merged):
        o_ref[0, r, :] = x + mod[2:3] * jnp.dot(m, wout_ref[0], preferred_element_type=F32)


def _mix(x, u, attn, layer, mods, mod_row, w_in, p, tm):
    b, t, d = x.shape
    per = tm // CONV_HALO
    last = t // CONV_HALO - 1
    seq = lambda width: pl.BlockSpec((1, tm, width), lambda bi, i: (bi, i, 0))
    weights = (p["norm1_g"], p["conv_dw"], p["conv_dw_b"], p["conv_ln_g"], p["conv_ln_b"],
               p["w_conv_out"], p["w_na_out"], p["w_out"])
    return pl.pallas_call(
        _mix_body,
        grid=(b, t // tm),
        in_specs=[
            seq(d),
            pl.BlockSpec((1, CONV_HALO, CONV_DIM), lambda bi, i: (bi, jnp.maximum(i * per - 1, 0), 0)),
            seq(CONV_DIM),
            pl.BlockSpec((1, CONV_HALO, CONV_DIM), lambda bi, i: (bi, jnp.minimum((i + 1) * per, last), 0)),
            seq(NA_DIM),
            _mod_spec(layer, mod_row),
            _whole_spec(w_in),
        ] + [_layer_spec(w, layer) for w in weights],
        out_specs=seq(d),
        out_shape=jax.ShapeDtypeStruct((b, t, d), F32),
        compiler_params=_params(("arbitrary", "arbitrary")),
        name="mix",
    )(x, u, u, u, attn, mods, w_in, *weights)


def _ffn_body(final_norm, x_ref, xp_ref, xn_ref, mod_ref, wup_ref, wdown_ref, g_ref, dw_ref, dwb_ref, fg_ref,
              o_ref):
    i = pl.program_id(1)
    tm = x_ref.shape[1]
    mod = mod_ref[0, 0]
    gain = g_ref[0]
    x = x_ref[0]
    hp = _rms_modulate(xp_ref[0], gain, mod[3:4], mod[4:5])
    hn = _rms_modulate(xn_ref[0], gain, mod[3:4], mod[4:5])
    hp = jnp.where(i > 0, hp, jnp.zeros_like(hp))
    hn = jnp.where(i < pl.num_programs(1) - 1, hn, jnp.zeros_like(hn))
    h = jnp.concatenate([hp, _rms_modulate(x, gain, mod[3:4], mod[4:5]), hn], axis=0).astype(BF16)
    n = tm + 2 * FFN_HALO
    assert FFN_CONV_K == 3

    def conv(cols, up):
        before = pltpu.roll(up, 1, 0)[FFN_HALO:FFN_HALO + tm]
        after = pltpu.roll(up, n - 1, 0)[FFN_HALO:FFN_HALO + tm]
        return (dwb_ref[0, :, cols] + dw_ref[0, 0:1, cols] * before
                + dw_ref[0, 1:2, cols] * up[FFN_HALO:FFN_HALO + tm] + dw_ref[0, 2:3, cols] * after)

    value_cols = slice(0, FFN_DIM)
    gate_cols = slice(FFN_DIM, 2 * FFN_DIM)
    up_gate = jnp.dot(h, wup_ref[:, gate_cols], preferred_element_type=F32)
    up_value = jnp.dot(h, wup_ref[:, value_cols], preferred_element_type=F32)
    act = (jax.nn.silu(conv(gate_cols, up_gate)) * conv(value_cols, up_value)).astype(BF16)
    y = x + mod[5:6] * jnp.dot(act, wdown_ref[...], preferred_element_type=F32)
    if final_norm:
        y = y * lax.rsqrt(jnp.mean(y * y, axis=-1, keepdims=True) + EPS) * fg_ref[...]
    o_ref[0] = y


def _ffn(x, layer, mods, mod_row, w_up, w_down, p, final_gain, final_norm, tm):
    b, t, d = x.shape
    per = tm // FFN_HALO
    last = t // FFN_HALO - 1
    seq = pl.BlockSpec((1, tm, d), lambda bi, i: (bi, i, 0))
    weights = (p["norm2_g"], p["ffn_dw"], p["ffn_dw_b"])
    return pl.pallas_call(
        functools.partial(_ffn_body, final_norm),
        grid=(b, t // tm),
        in_specs=[
            seq,
            pl.BlockSpec((1, FFN_HALO, d), lambda bi, i: (bi, jnp.maximum(i * per - 1, 0), 0)),
            pl.BlockSpec((1, FFN_HALO, d), lambda bi, i: (bi, jnp.minimum((i + 1) * per, last), 0)),
            _mod_spec(layer, mod_row),
            _whole_spec(w_up),
            _whole_spec(w_down),
        ] + [_layer_spec(w, layer) for w in weights] + [_whole_spec(final_gain)],
        out_specs=seq,
        out_shape=jax.ShapeDtypeStruct((b, t, d), F32),
        compiler_params=_params(("arbitrary", "arbitrary")),
        name="ffn",
    )(x, x, x, mods, w_up, w_down, *weights, final_gain)


def _rope_tables(seq_len):
    half = HEAD_DIM // 2
    inv_freq = ROPE_THETA ** (-jnp.arange(0, half, 2, dtype=F32) / half)
    t = jnp.arange(seq_len)
    ang_r = (t // GRID_W).astype(F32)[:, None] * inv_freq[None, :]
    ang_c = (t % GRID_W).astype(F32)[:, None] * inv_freq[None, :]
    cos = jnp.concatenate([jnp.cos(ang_r)] * 2 + [jnp.cos(ang_c)] * 2, axis=-1)
    sin = jnp.concatenate([-jnp.sin(ang_r), jnp.sin(ang_r), -jnp.sin(ang_c), jnp.sin(ang_c)], axis=-1)
    return jnp.tile(cos, (1, HEADS_PER_BLOCK)), jnp.tile(sin, (1, HEADS_PER_BLOCK))


def kernel(x, c, ctx, c_ctx, w_ada, b_ada, norm1_g, w_in, conv_dw, conv_dw_b, conv_ln_g, conv_ln_b, w_conv_out,
           na_rpb, w_na_out, w_out, norm2_g, w_up, ffn_dw, ffn_dw_b, w_down, final_norm_g):
    batch, seq_len, d = x.shape
    depth = w_ada.shape[0]
    ctx_len = ctx.shape[1]
    assert d == D_MODEL and seq_len % GRID_W == 0 and batch + 1 <= MOD_ROWS

    cvec = jnp.zeros((MOD_ROWS, d), F32).at[:batch].set(c).at[batch].set(c_ctx)
    mods, bias, (w_conv_bf, w_na_bf, w_out_bf) = _ada_and_window_bias(
        cvec, w_ada, b_ada, na_rpb, (w_conv_out, w_na_out, w_out))
    mods = mods.reshape(depth, MOD_ROWS, N_MOD, d)
    lat_row = lambda bi: bi
    ctx_row = lambda bi: batch

    cos, sin = _rope_tables(seq_len)
    cos_id = jnp.ones((ctx_len, LANES), F32)
    sin_id = jnp.zeros((ctx_len, LANES), F32)
    rows = lambda v: v.reshape(depth, 1, -1)
    p = {
        "norm1_g": rows(norm1_g), "norm2_g": rows(norm2_g),
        "conv_dw": conv_dw, "conv_dw_b": rows(conv_dw_b), "conv_ln_g": rows(conv_ln_g), "conv_ln_b": rows(conv_ln_b),
        "ffn_dw": ffn_dw, "ffn_dw_b": rows(ffn_dw_b),
        "w_conv_out": w_conv_bf, "w_na_out": w_na_bf, "w_out": w_out_bf,
    }
    fin = final_norm_g.reshape(1, d)

    xc = ctx
    for l in range(depth):
        last = l == depth - 1
        u_c, q_c, k_c, v_c, w_in_l = _inproj(xc, l, mods, ctx_row, p["norm1_g"], cos_id, sin_id, w_in, ctx_len)
        u, q, k, v = _inproj(x, l, mods, lat_row, p["norm1_g"], cos, sin, w_in_l, INPROJ_TM)
        attn, w_up_l, w_down_l = _neighbourhood_attention(q, k, v, k_c, v_c, bias, w_up, w_down, l)
        x = _mix(x, u, attn, l, mods, lat_row, w_in_l, p, MIX_TM)
        x = _ffn(x, l, mods, lat_row, w_up_l, w_down_l, p, fin, last, FFN_TM)
        if not last:
            attn_c = _context_attention(q_c, k_c, v_c)
            xc = _mix(xc, u_c, attn_c, l, mods, ctx_row, w_in_l, p, ctx_len)
            xc = _ffn(xc, l, mods, ctx_row, w_up_l, w_down_l, p, fin, False, ctx_len)
    return x
```
